```python
import math
import jax
import jax.numpy as jnp
from jax import lax
import numpy as np

D_MODEL = 1024
BATCH = 8
SEQ = 2048
DEPTH = 4
DEC_BATCH = 128
DEC_SEQ = 1
PAST_LEN = 2048
PAGE_SIZE = 128

N_EVEN = (DEPTH + 1) // 2
N_ODD = DEPTH // 2
CHUNK = 128
A_HEADS = D_MODEL // 256
A_WIDTH = D_MODEL // 2
A_DH = A_WIDTH // A_HEADS
B_HEADS = D_MODEL // 256
B_DK = 128
B_DV = 128
B_WIDTH = B_HEADS * B_DV
CONV_W = 4
DN_CHUNK = 64
C_HEADS = D_MODEL // 64
C_DH = 64
C_WIDTH = C_HEADS * C_DH
Q_BLOCK = 128
FGATE_BIAS = 3.0
N_EXPERTS = 32
TOP_K = 4
D_FF = D_MODEL
SWIGLU_ALPHA = 1.702
SWIGLU_LIMIT = 7.0
MOE_BLOCK = 128
EPS = 1e-6

EVEN_IN = 2 * A_WIDTH + 4 * B_WIDTH + 2 * B_HEADS
EVEN_SPLITS = [A_WIDTH, 2 * A_WIDTH, 2 * A_WIDTH + 3 * B_WIDTH, 2 * A_WIDTH + 4 * B_WIDTH,
               2 * A_WIDTH + 4 * B_WIDTH + B_HEADS]
ODD_IN = 3 * C_WIDTH + C_HEADS
ODD_SPLITS = [C_WIDTH, 2 * C_WIDTH, 3 * C_WIDTH]

kernel_name = "hybrid_gmlp_deltanet_fox_moe_adaln_step"

F32 = jnp.float32


def rms_norm(x, g):
    xf = x.astype(F32)
    y = xf * lax.rsqrt(jnp.mean(xf * xf, axis=-1, keepdims=True) + EPS)
    return (y * g.astype(F32)).astype(x.dtype)


def l2norm(x):
    return x * lax.rsqrt(jnp.sum(x * x, axis=-1, keepdims=True) + EPS)


def ada_mod(c, w, b):
    m = (jax.nn.silu(c) @ w + b)[:, None, :]
    return jnp.split(m, 6, axis=-1)


def chunk_gmlp(u, v, ln_g, ln_b, ws, bs):
    Bn, S, _ = u.shape
    n = -(-S // CHUNK)
    pad = n * CHUNK - S
    u = jax.nn.gelu(u, approximate=False)
    vh = jax.nn.gelu(v, approximate=False).reshape(Bn, S, A_HEADS, A_DH).astype(F32)
    mu = jnp.mean(vh, axis=-1, keepdims=True)
    var = jnp.mean(jnp.square(vh - mu), axis=-1, keepdims=True)
    vh = ((vh - mu) * lax.rsqrt(var + EPS) * ln_g.reshape(A_HEADS, A_DH).astype(F32)
          + ln_b.reshape(A_HEADS, A_DH).astype(F32)).astype(u.dtype)
    v_rows = vh.reshape(Bn, S, A_WIDTH)
    vp = jnp.pad(vh, ((0, 0), (0, pad), (0, 0), (0, 0))).reshape(Bn, n, CHUNK, A_HEADS, A_DH)
    tril = jnp.tril(jnp.ones((CHUNK, CHUNK), dtype=bool))
    wm = jnp.where(tril[None], ws, 0.0)
    mixed = jnp.einsum('hts,bnshc->bnthc', wm, vp) + jnp.swapaxes(bs, 0, 1)[:, :, None]
    mixed = mixed.reshape(Bn, n * CHUNK, A_WIDTH)[:, :S]
    return u * mixed, v_rows


def gated_delta_chunked(q, k, v, g, beta, S0):
    Bn, S, H, Dk = q.shape
    Dv = v.shape[-1]
    n = S // DN_CHUNK

    def to_chunks(t):
        t = t.reshape((Bn, n, DN_CHUNK, H) + t.shape[3:])
        return jnp.moveaxis(t, 3, 1)

    qc = to_chunks(q * (Dk ** -0.5))
    kc = to_chunks(k)
    vc = to_chunks(v)
    gc = jnp.cumsum(to_chunks(g), axis=-1)
    bc = to_chunks(beta)
    idx = jnp.arange(DN_CHUNK)
    incl = idx[:, None] >= idx[None, :]
    strict = idx[:, None] > idx[None, :]
    decay = jnp.exp(jnp.where(incl, gc[..., :, None] - gc[..., None, :], -jnp.inf))
    kb = kc * bc[..., None]
    L = jnp.where(strict, jnp.einsum('bhnid,bhnjd->bhnij', kb, kc) * decay, 0.0)
    eye = jnp.eye(DN_CHUNK, dtype=F32)
    T = lax.linalg.triangular_solve(L + eye, jnp.broadcast_to(eye, L.shape),
                                    left_side=True, lower=True, unit_diagonal=True)
    value = T @ (vc * bc[..., None])
    k_cumdecay = T @ (kb * jnp.exp(gc)[..., None])
    qk = jnp.where(incl, jnp.einsum('bhnid,bhnjd->bhnij', qc, kc) * decay, 0.0)
    qg = qc * jnp.exp(gc)[..., None]
    kdec = kc * jnp.exp(gc[..., -1:] - gc)[..., None]
    last = jnp.exp(gc[..., -1])

    def step(St, xs):
        qg_i, qk_i, val_i, kcd_i, kdec_i, last_i = xs
        v_new = val_i - jnp.einsum('bhck,bhkv->bhcv', kcd_i, St)
        o = jnp.einsum('bhck,bhkv->bhcv', qg_i, St) + jnp.einsum('bhcs,bhsv->bhcv', qk_i, v_new)
        St = St * last_i[..., None, None] + jnp.einsum('bhck,bhcv->bhkv', kdec_i, v_new)
        return St, o

    xs = tuple(jnp.moveaxis(t, 2, 0) for t in (qg, qk, value, k_cumdecay, kdec, last))
    S_fin, o = lax.scan(step, S0, xs)
    o = jnp.moveaxis(jnp.moveaxis(o, 0, 2), 1, 3).reshape(Bn, S, H, Dv)
    return o, S_fin


def gated_delta_recurrent(q, k, v, g, beta, S0):
    Dk = q.shape[-1]

    def step(St, xs):
        q_t, k_t, v_t, g_t, b_t = xs
        St = St * jnp.exp(g_t)[..., None, None]
        delta = (v_t - jnp.einsum('bhk,bhkv->bhv', k_t, St)) * b_t[..., None]
        St = St + k_t[..., :, None] * delta[..., None, :]
        return St, jnp.einsum('bhk,bhkv->bhv', q_t, St)

    xs = tuple(jnp.moveaxis(t, 1, 0) for t in (q * (Dk ** -0.5), k, v, g, beta))
    S_fin, o = lax.scan(step, S0, xs)
    return jnp.moveaxis(o, 0, 1), S_fin


def even_mixer(h, conv_buf, S0, per_token, w_in, ln_g, ln_b, ws, bs, cw, a_log, dtb, og, w_out):
    Bn, S, _ = h.shape
    z = h @ w_in
    a_u, a_v, qkv_raw, gz, ga, gb = jnp.split(z, EVEN_SPLITS, axis=-1)
    y_a, v_rows = chunk_gmlp(a_u, a_v, ln_g, ln_b, ws, bs)
    xc = jnp.concatenate([conv_buf.astype(qkv_raw.dtype), qkv_raw], axis=1)
    new_buf = xc[:, xc.shape[1] - (CONV_W - 1):]
    qkv = jax.nn.silu(sum(xc[:, j:j + S] * cw[j] for j in range(CONV_W)))
    q, k, v = jnp.split(qkv.astype(F32).reshape(Bn, S, 3 * B_HEADS, B_DK), 3, axis=2)
    q = l2norm(q)
    k = l2norm(k)
    beta = jax.nn.sigmoid(gb.astype(F32))
    g = -jnp.exp(a_log.astype(F32)) * jax.nn.softplus(ga.astype(F32) + dtb.astype(F32))
    S0 = S0.astype(F32)
    if per_token:
        o, S_new = gated_delta_recurrent(q, k, v, g, beta, S0)
    else:
        o, S_new = gated_delta_chunked(q, k, v, g, beta, S0)
    o = rms_norm(o.astype(h.dtype), og) * jax.nn.silu(gz.reshape(Bn, S, B_HEADS, B_DV))
    y = jnp.concatenate([y_a, o.reshape(Bn, S, B_WIDTH)], axis=-1) @ w_out
    return y, v_rows, new_buf, S_new


def fox_project(h, w_in, b_f, qg, kg):
    Bn, S, _ = h.shape
    z = h @ w_in
    q, k, v, f = jnp.split(z, ODD_SPLITS, axis=-1)
    q = rms_norm(q.reshape(Bn, S, C_HEADS, C_DH), qg)
    k = rms_norm(k.reshape(Bn, S, C_HEADS, C_DH), kg)
    v = v.reshape(Bn, S, C_HEADS, C_DH)
    logf = jax.nn.log_sigmoid(f.astype(F32) + b_f.astype(F32))
    return q, k, v, logf


def fox_attention(q, k, v, Fq, Fk, q_pos, k_pos):
    Bn, Sq, H, Dh = q.shape
    qb = min(Q_BLOCK, Sq)
    nb = -(-Sq // qb)
    pad = nb * qb - Sq
    q = jnp.pad(q, ((0, 0), (0, pad), (0, 0), (0, 0)))
    Fq = jnp.pad(Fq, ((0, 0), (0, pad), (0, 0)))
    q_pos = jnp.pad(q_pos, (0, pad), mode='edge')
    qs = jnp.moveaxis(q.reshape(Bn, nb, qb, H, Dh), 1, 0)
    Fqs = jnp.moveaxis(Fq.reshape(Bn, nb, qb, H), 1, 0)
    ps = q_pos.reshape(nb, qb)
    Fk_t = jnp.swapaxes(Fk, 1, 2)
    scale = Dh ** -0.5

    def block(args):
        q_i, F_i, p_i = args
        s = jnp.einsum('bqhd,bkhd->bhqk', q_i, k).astype(F32) * scale
        s = s + (jnp.swapaxes(F_i, 1, 2)[..., :, None] - Fk_t[..., None, :])
        s = jnp.where(k_pos[None, :] <= p_i[:, None], s, -jnp.inf)
        p = jax.nn.softmax(s, axis=-1).astype(v.dtype)
        return jnp.einsum('bhqk,bkhd->bqhd', p, v)

    o = lax.map(block, (qs, Fqs, ps))
    return jnp.moveaxis(o, 0, 1).reshape(Bn, nb * qb, H, Dh)[:, :Sq]


def clamped_swiglu(z):
    glu, lin = z[..., :D_FF], z[..., D_FF:]
    glu = jnp.minimum(glu, SWIGLU_LIMIT)
    lin = jnp.clip(lin, -SWIGLU_LIMIT, SWIGLU_LIMIT)
    return glu * jax.nn.sigmoid(SWIGLU_ALPHA * glu) * (lin + 1.0)


def moe_ffn(h, w_r, b_r, w1, b1, w2, b2):
    T, D = h.shape
    logits = (h @ w_r).astype(F32) + b_r.astype(F32)
    top_v, top_i = lax.top_k(logits, TOP_K)
    gate = jax.nn.softmax(top_v, axis=-1)
    n = T * TOP_K
    flat_e = top_i.reshape(-1)
    flat_t = jnp.repeat(jnp.arange(T, dtype=jnp.int32), TOP_K)
    flat_g = gate.reshape(-1)
    order = jnp.argsort(flat_e)
    se, st, sg = flat_e[order], flat_t[order], flat_g[order]
    counts = jnp.bincount(flat_e, length=N_EXPERTS)
    padded = (counts + MOE_BLOCK - 1) // MOE_BLOCK * MOE_BLOCK
    pad_end = jnp.cumsum(padded)
    pad_start = pad_end - padded
    start = jnp.cumsum(counts) - counts
    dest = pad_start[se] + (jnp.arange(n) - start[se])
    nb = -(-(n + N_EXPERTS * (MOE_BLOCK - 1)) // MOE_BLOCK)
    slot_tok = jnp.full((nb * MOE_BLOCK,), T, jnp.int32).at[dest].set(st)
    blk_e = jnp.minimum(jnp.searchsorted(pad_end, jnp.arange(nb) * MOE_BLOCK, side='right'), N_EXPERTS - 1)
    h_pad = jnp.concatenate([h, jnp.zeros((1, D), h.dtype)], axis=0)
    xb = h_pad[slot_tok].reshape(nb, MOE_BLOCK, D)

    def expert_block(args):
        x_blk, e = args
        zz = x_blk @ w1[e] + b1[e]
        return clamped_swiglu(zz) @ w2[e] + b2[e]

    yb = lax.map(expert_block, (xb, blk_e)).reshape(nb * MOE_BLOCK, D)
    y_assign = yb[dest] * sg[:, None].astype(yb.dtype)
    return jax.ops.segment_sum(y_assign, st, num_segments=T)


def setup_inputs(seed: int = 0) -> dict:
    key = jax.random.key(seed)
    kit = iter(jax.random.split(key, 40))
    n_pages = PAST_LEN // PAGE_SIZE
    n_pool = (5 * DEC_BATCH * n_pages) // 4

    def nrm(shape, scale):
        return jax.random.normal(next(kit), shape, F32) * scale

    inp = {}
    inp["x_prompt"] = nrm((BATCH, SEQ, D_MODEL), 1.0)
    inp["x_sample"] = nrm((DEC_BATCH, DEC_SEQ, D_MODEL), 1.0)
    inp["cache_k"] = nrm((N_ODD, n_pool, PAGE_SIZE, C_HEADS, C_DH), 1.0)
    inp["cache_v"] = nrm((N_ODD, n_pool, PAGE_SIZE, C_HEADS, C_DH), 1.0)
    inp["cache_logf"] = jax.nn.log_sigmoid(FGATE_BIAS + nrm((N_ODD, n_pool, PAGE_SIZE, C_HEADS), 1.0))
    inp["state_delta"] = nrm((N_EVEN, DEC_BATCH, B_HEADS, B_DK, B_DV), B_DK ** -0.5)
    inp["state_conv"] = nrm((N_EVEN, DEC_BATCH, CONV_W - 1, 3 * B_WIDTH), 1.0)
    inp["page_table"] = jax.random.permutation(next(kit), n_pool)[:DEC_BATCH * n_pages].reshape(
        DEC_BATCH, n_pages).astype(jnp.int32)
    inp["c_prompt"] = nrm((BATCH, D_MODEL), 1.0)
    inp["c_sample"] = nrm((DEC_BATCH, D_MODEL), 1.0)
    inp["norm_g"] = 1.0 + nrm((DEPTH, 2, D_MODEL), 0.05)
    inp["w_ada"] = nrm((DEPTH, D_MODEL, 6 * D_MODEL), 0.5 * D_MODEL ** -0.5)
    inp["b_ada"] = nrm((DEPTH, 6 * D_MODEL), 0.01)
    inp["w_in_even"] = nrm((N_EVEN, D_MODEL, EVEN_IN), D_MODEL ** -0.5)
    inp["a_ln_g"] = 1.0 + nrm((N_EVEN, A_WIDTH), 0.05)
    inp["a_ln_b"] = nrm((N_EVEN, A_WIDTH), 0.01)
    inp["a_ws"] = nrm((N_EVEN, A_HEADS, CHUNK, CHUNK), CHUNK ** -0.5)
    inp["a_bs"] = 1.0 + nrm((N_EVEN, A_HEADS, CHUNK), 0.1)
    inp["conv_w"] = nrm((N_EVEN, CONV_W, 3 * B_WIDTH), CONV_W ** -0.5)
    inp["A_log"] = jnp.log(jax.random.uniform(next(kit), (N_EVEN, B_HEADS), F32, 1.0, 16.0))
    dt = jnp.exp(jax.random.uniform(next(kit), (N_EVEN, B_HEADS), F32, math.log(1e-3), math.log(1e-1)))
    inp["dt_bias"] = dt + jnp.log(-jnp.expm1(-dt))
    inp["onorm_g"] = 1.0 + nrm((N_EVEN, B_DV), 0.05)
    inp["w_out_even"] = nrm((N_EVEN, A_WIDTH + B_WIDTH, D_MODEL), (A_WIDTH + B_WIDTH) ** -0.5)
    inp["w_in_odd"] = nrm((N_ODD, D_MODEL, ODD_IN), D_MODEL ** -0.5)
    inp["b_f"] = FGATE_BIAS + nrm((N_ODD, C_HEADS), 0.5)
    inp["q_norm_g"] = 1.0 + nrm((N_ODD, C_DH), 0.05)
    inp["k_norm_g"] = 1.0 + nrm((N_ODD, C_DH), 0.05)
    inp["w_out_odd"] = nrm((N_ODD, C_WIDTH, D_MODEL), C_WIDTH ** -0.5)
    inp["w_router"] = nrm((DEPTH, D_MODEL, N_EXPERTS), D_MODEL ** -0.5)
    inp["b_router"] = nrm((DEPTH, N_EXPERTS), 0.01)
    inp["w_mlp1"] = nrm((DEPTH, N_EXPERTS, D_MODEL, 2 * D_FF), D_MODEL ** -0.5)
    inp["b_mlp1"] = nrm((DEPTH, N_EXPERTS, 2 * D_FF), 0.01)
    inp["w_mlp2"] = nrm((DEPTH, N_EXPERTS, D_FF, D_MODEL), D_FF ** -0.5)
    inp["b_mlp2"] = nrm((DEPTH, N_EXPERTS, D_MODEL), 0.01)
    return inp


def reference(x_prompt, x_sample, cache_k, cache_v, cache_logf, state_delta, state_conv, page_table,
              c_prompt, c_sample, norm_g, w_ada, b_ada, w_in_even, a_ln_g, a_ln_b, a_ws, a_bs, conv_w,
              A_log, dt_bias, onorm_g, w_out_even, w_in_odd, b_f, q_norm_g, k_norm_g, w_out_odd,
              w_router, b_router, w_mlp1, b_mlp1, w_mlp2, b_mlp2):
    xp, xs = x_prompt, x_sample
    Bp, Sp, D = xp.shape
    Bs, Ss, _ = xs.shape
    past_len = page_table.shape[1] * cache_k.shape[2]
    pos_p = jnp.arange(Sp)
    pos_q_s = past_len + jnp.arange(Ss)
    pos_k_s = jnp.arange(past_len + Ss)
    kp_l, vp_l, fp_l, ks_l, vs_l, fs_l = [], [], [], [], [], []
    dp_l, cp_l, ds_l, cs_l, av_l = [], [], [], [], []

    for l in range(DEPTH):
        i = l // 2
        sh1p, sc1p, g1p, sh2p, sc2p, g2p = ada_mod(c_prompt, w_ada[l], b_ada[l])
        sh1s, sc1s, g1s, sh2s, sc2s, g2s = ada_mod(c_sample, w_ada[l], b_ada[l])
        hp = rms_norm(xp, norm_g[l, 0]) * (1.0 + sc1p) + sh1p
        hs = rms_norm(xs, norm_g[l, 0]) * (1.0 + sc1s) + sh1s
        if l % 2 == 0:
            ew = (w_in_even[i], a_ln_g[i], a_ln_b[i], a_ws[i], a_bs[i], conv_w[i], A_log[i], dt_bias[i],
                  onorm_g[i], w_out_even[i])
            zero_buf = jnp.zeros((Bp, CONV_W - 1, 3 * B_WIDTH), xp.dtype)
            zero_S = jnp.zeros((Bp, B_HEADS, B_DK, B_DV), F32)
            op, _, cbp, sfp = even_mixer(hp, zero_buf, zero_S, False, *ew)
            os_, av, cbs, sfs = even_mixer(hs, state_conv[i], state_delta[i], True, *ew)
            dp_l.append(sfp)
            cp_l.append(cbp)
            ds_l.append(sfs)
            cs_l.append(cbs)
            av_l.append(av)
        else:
            qp, kp, vp, lfp = fox_project(hp, w_in_odd[i], b_f[i], q_norm_g[i], k_norm_g[i])
            Fp = jnp.cumsum(lfp, axis=1)
            op = fox_attention(qp, kp, vp, Fp, Fp, pos_p, pos_p).reshape(Bp, Sp, C_WIDTH) @ w_out_odd[i]
            qs, ks, vs, lfs = fox_project(hs, w_in_odd[i], b_f[i], q_norm_g[i], k_norm_g[i])
            pk = cache_k[i, page_table].reshape(Bs, past_len, C_HEADS, C_DH)
            pv = cache_v[i, page_table].reshape(Bs, past_len, C_HEADS, C_DH)
            plf = cache_logf[i, page_table].reshape(Bs, past_len, C_HEADS)
            k_all = jnp.concatenate([pk.astype(ks.dtype), ks], axis=1)
            v_all = jnp.concatenate([pv.astype(vs.dtype), vs], axis=1)
            F_all = jnp.cumsum(jnp.concatenate([plf.astype(F32), lfs], axis=1), axis=1)
            os_ = fox_attention(qs, k_all, v_all, F_all[:, past_len:], F_all, pos_q_s, pos_k_s)
            os_ = os_.reshape(Bs, Ss, C_WIDTH) @ w_out_odd[i]
            kp_l.append(kp)
            vp_l.append(vp)
            fp_l.append(lfp)
            ks_l.append(ks)
            vs_l.append(vs)
            fs_l.append(lfs)
        xp = xp + g1p * op
        xs = xs + g1s * os_
        hp = rms_norm(xp, norm_g[l, 1]) * (1.0 + sc2p) + sh2p
        hs = rms_norm(xs, norm_g[l, 1]) * (1.0 + sc2s) + sh2s
        mw = (w_router[l], b_router[l], w_mlp1[l], b_mlp1[l], w_mlp2[l], b_mlp2[l])
        xp = xp + g2p * moe_ffn(hp.reshape(Bp * Sp, D), *mw).reshape(Bp, Sp, D)
        xs = xs + g2s * moe_ffn(hs.reshape(Bs * Ss, D), *mw).reshape(Bs, Ss, D)

    k_new_prompt = jnp.stack(kp_l)
    v_new_prompt = jnp.stack(vp_l)
    logf_new_prompt = jnp.stack(fp_l)
    k_new_sample = jnp.stack(ks_l)
    v_new_sample = jnp.stack(vs_l)
    logf_new_sample = jnp.stack(fs_l)
    delta_prompt = jnp.stack(dp_l)
    conv_prompt = jnp.stack(cp_l)
    delta_sample = jnp.stack(ds_l)
    conv_sample = jnp.stack(cs_l)
    chunk_v_sample = jnp.stack(av_l)
    return (xp, xs, k_new_prompt, v_new_prompt, logf_new_prompt, k_new_sample, v_new_sample, logf_new_sample,
            delta_prompt, conv_prompt, delta_sample, conv_sample, chunk_v_sample)
```

```python
import functools

import jax
import jax.numpy as jnp
from jax import lax
from jax.experimental import pallas as pl
from jax.experimental.pallas import tpu as pltpu

F32 = jnp.float32
BF16 = jnp.bfloat16

EPS = 1e-6
CHUNK = 128
DN_CHUNK = 64
N_EXPERTS = 32
TOP_K = 4
SWIGLU_ALPHA = 1.702
SWIGLU_LIMIT = 7.0
CONV_W = 4
LANE = 128
NEG = -1e30

VMEM_LIMIT = 56 * 1024 * 1024
ROW_TILE = 512
MOE_TILE = 256
PAGES_PER_STEP = 4


def _params(n_axes):
    return pltpu.CompilerParams(dimension_semantics=("arbitrary",) * n_axes,
                                vmem_limit_bytes=VMEM_LIMIT)


def _dot(a, b):
    return jnp.dot(a, b, preferred_element_type=F32)


def _dot_nt(a, b):
    return lax.dot_general(a, b, (((1,), (1,)), ((), ())), preferred_element_type=F32)


def _dot_tn(a, b):
    return lax.dot_general(a, b, (((0,), (0,)), ((), ())), preferred_element_type=F32)


def _split2(a):
    hi = a.astype(BF16)
    lo = (a - hi.astype(F32)).astype(BF16)
    return hi, lo


def _split3(a):
    hi = a.astype(BF16)
    r = a - hi.astype(F32)
    mid = r.astype(BF16)
    lo = (r - mid.astype(F32)).astype(BF16)
    return hi, mid, lo


def _dot3(a, b):
    ah, al = _split2(a)
    bh, bl = _split2(b)
    return _dot(ah, bh) + (_dot(ah, bl) + _dot(al, bh))


def _dot_sel(sel_bf16, x):
    hi, mid, lo = _split3(x)
    return _dot(sel_bf16, hi) + (_dot(sel_bf16, mid) + _dot(sel_bf16, lo))


def _x_dot_sel(x, sel_bf16):
    hi, mid, lo = _split3(x)
    return _dot(hi, sel_bf16) + (_dot(mid, sel_bf16) + _dot(lo, sel_bf16))


def _rms_mod(x, g, sc, sh):
    ms = jnp.mean(x * x, axis=-1, keepdims=True)
    return (x * lax.rsqrt(ms + EPS) * g) * (1.0 + sc) + sh


def _gelu(x):
    return 0.5 * x * (1.0 + lax.erf(x * 0.7071067811865476))


def _silu(x):
    return x * jax.nn.sigmoid(x)


def _cast_rows(src_ref, dst_ref, rows=128):
    n = src_ref.shape[0] // rows

    def body(r, c):
        sl = pl.ds(pl.multiple_of(r * rows, rows), rows)
        dst_ref[sl, :] = src_ref[sl, :].astype(BF16)
        return c

    lax.fori_loop(0, n, body, 0)


def _ada_kernel(c_ref, w_ref, b_ref, o_ref):
    c = c_ref[...]
    o_ref[0] = _dot3(_silu(c), w_ref[0]) + b_ref[0]


def _ada_all(c_all, w_ada, b_ada):
    L, D, N = w_ada.shape
    nb = c_all.shape[0]
    tn = 1536
    return pl.pallas_call(
        _ada_kernel,
        grid=(L, N // tn),
        in_specs=[pl.BlockSpec((nb, D), lambda l, j: (0, 0)),
                  pl.BlockSpec((1, D, tn), lambda l, j: (l, 0, j)),
                  pl.BlockSpec((1, 1, tn), lambda l, j: (l, 0, j))],
        out_specs=pl.BlockSpec((1, nb, tn), lambda l, j: (l, 0, j)),
        out_shape=jax.ShapeDtypeStruct((L, nb, N), F32),
        compiler_params=_params(2),
        name="ada_mod",
    )(c_all, w_ada, b_ada.reshape(L, 1, N))


def _proj_kernel(x_ref, g_ref, sc_ref, sh_ref, w_ref, wt_ref, o_ref, ot_ref, wbf, wtbf, *, n_chunk):
    @pl.when(pl.program_id(0) == 0)
    def _():
        _cast_rows(w_ref, wbf)
        wtbf[...] = wt_ref[...].astype(BF16)

    h = _rms_mod(x_ref[...], g_ref[...], sc_ref[0], sh_ref[0]).astype(BF16)
    for n0 in range(0, o_ref.shape[1], n_chunk):
        o_ref[:, n0:n0 + n_chunk] = _dot(h, wbf[:, n0:n0 + n_chunk])
    ot_ref[...] = _dot(h, wtbf[...])


def _proj(x, g, sc, sh, w, w_tail, *, tm, rows_per_mod):
    M, D = x.shape
    N = w.shape[1]
    NT = w_tail.shape[1]
    bpm = rows_per_mod // tm
    R = sc.shape[1]
    mod = pl.BlockSpec((1, R, D), lambda i: (i // bpm, 0, 0))
    return pl.pallas_call(
        functools.partial(_proj_kernel, n_chunk=512),
        grid=(M // tm,),
        in_specs=[pl.BlockSpec((tm, D), lambda i: (i, 0)),
                  pl.BlockSpec((1, D), lambda i: (0, 0)),
                  mod, mod,
                  pl.BlockSpec((D, N), lambda i: (0, 0), pipeline_mode=pl.Buffered(1)),
                  pl.BlockSpec((D, NT), lambda i: (0, 0), pipeline_mode=pl.Buffered(1))],
        out_specs=[pl.BlockSpec((tm, N), lambda i: (i, 0)),
                   pl.BlockSpec((tm, NT), lambda i: (i, 0))],
        out_shape=[jax.ShapeDtypeStruct((M, N), F32), jax.ShapeDtypeStruct((M, NT), F32)],
        scratch_shapes=[pltpu.VMEM((D, N), BF16), pltpu.VMEM((D, NT), BF16)],
        compiler_params=_params(1),
        name="norm_mod_proj",
    )(x, g, sc, sh, w, w_tail)


def _outproj_kernel(*refs, n_in):
    ys = refs[:n_in]
    w_ref, x_ref, gate_ref, o_ref, wbf = refs[n_in:]

    @pl.when(pl.program_id(0) == 0)
    def _():
        _cast_rows(w_ref, wbf)

    acc = None
    k0 = 0
    for y_ref in ys:
        kk = y_ref.shape[1]
        t = _dot(y_ref[...].astype(BF16), wbf[k0:k0 + kk, :])
        acc = t if acc is None else acc + t
        k0 += kk
    o_ref[...] = x_ref[...] + gate_ref[0] * acc


def _outproj(ys, w, x, gate, *, tm, rows_per_mod):
    M, D = x.shape
    K = w.shape[0]
    bpm = rows_per_mod // tm
    R = gate.shape[1]
    in_specs = [pl.BlockSpec((tm, y.shape[1]), lambda i: (i, 0)) for y in ys]
    in_specs += [pl.BlockSpec((K, D), lambda i: (0, 0), pipeline_mode=pl.Buffered(1)),
                 pl.BlockSpec((tm, D), lambda i: (i, 0)),
                 pl.BlockSpec((1, R, D), lambda i: (i // bpm, 0, 0))]
    return pl.pallas_call(
        functools.partial(_outproj_kernel, n_in=len(ys)),
        grid=(M // tm,),
        in_specs=in_specs,
        out_specs=pl.BlockSpec((tm, D), lambda i: (i, 0)),
        out_shape=jax.ShapeDtypeStruct((M, D), F32),
        scratch_shapes=[pltpu.VMEM((K, D), BF16)],
        compiler_params=_params(1),
        name="out_proj_residual",
    )(*ys, w, x, gate)


def _gmlp_kernel(u_ref, v_ref, lng_ref, lnb_ref, ws_ref, bst_ref, o_ref, *, n_heads):
    tm = u_ref.shape[0]
    u = _gelu(u_ref[...])
    v = _gelu(v_ref[...])
    r = lax.broadcasted_iota(jnp.int32, (CHUNK, CHUNK), 0)
    c = lax.broadcasted_iota(jnp.int32, (CHUNK, CHUNK), 1)
    tril = r >= c
    for h in range(n_heads):
        ls = slice(h * LANE, (h + 1) * LANE)
        vh = v[:, ls]
        mu = jnp.mean(vh, axis=-1, keepdims=True)
        d = vh - mu
        var = jnp.mean(d * d, axis=-1, keepdims=True)
        vn = (d * lax.rsqrt(var + EPS) * lng_ref[:, ls] + lnb_ref[:, ls]).astype(BF16)
        wm = jnp.where(tril, ws_ref[h], 0.0).astype(BF16)
        bias = bst_ref[:, h:h + 1]
        for c0 in range(0, tm, CHUNK):
            mixed = _dot(wm, vn[c0:c0 + CHUNK]) + bias
            o_ref[c0:c0 + CHUNK, ls] = (u[c0:c0 + CHUNK, ls] * mixed).astype(BF16)


def _gmlp(z, ln_g, ln_b, ws, bs, *, tm):
    M = z.shape[0]
    H = ws.shape[0]
    W = H * LANE
    return pl.pallas_call(
        functools.partial(_gmlp_kernel, n_heads=H),
        grid=(M // tm,),
        in_specs=[pl.BlockSpec((tm, W), lambda i: (i, 0)),
                  pl.BlockSpec((tm, W), lambda i: (i, 1)),
                  pl.BlockSpec((1, W), lambda i: (0, 0)),
                  pl.BlockSpec((1, W), lambda i: (0, 0)),
                  pl.BlockSpec((H, CHUNK, CHUNK), lambda i: (0, 0, 0)),
                  pl.BlockSpec((CHUNK, H), lambda i: (0, 0))],
        out_specs=pl.BlockSpec((tm, W), lambda i: (i, 0)),
        out_shape=jax.ShapeDtypeStruct((M, W), BF16),
        compiler_params=_params(1),
        name="gmlp_chunk",
    )(z, z, ln_g.reshape(1, W), ln_b.reshape(1, W), ws, bs.T)


def _delta_kernel(zq_ref, zk_ref, zv_ref, zg_ref, gz_ref, cw_ref, alog_ref, dtb_ref, og_ref,
                  o_ref, sfin_ref, xbuf, state, *, n_heads):
    i = pl.program_id(1)
    tm = zq_ref.shape[0]
    W = n_heads * LANE
    C = DN_CHUNK

    @pl.when(i == 0)
    def _():
        xbuf[0:8, :] = jnp.zeros((8, 3 * W), F32)
        state[...] = jnp.zeros(state.shape, F32)

    xbuf[8:8 + tm, 0:W] = zq_ref[...]
    xbuf[8:8 + tm, W:2 * W] = zk_ref[...]
    xbuf[8:8 + tm, 2 * W:3 * W] = zv_ref[...]
    conv = xbuf[8:8 + tm, :] * cw_ref[3:4, :]
    for s in range(1, CONV_W):
        conv = conv + xbuf[8 - s:8 - s + tm, :] * cw_ref[3 - s:4 - s, :]
    tail = xbuf[tm:tm + 8, :]
    xbuf[0:8, :] = tail
    qkv = _silu(conv)

    zg = zg_ref[...]
    g_all = -jnp.exp(alog_ref[...]) * jax.nn.softplus(zg + dtb_ref[...])
    beta_all = jax.nn.sigmoid(zg)

    r = lax.broadcasted_iota(jnp.int32, (C, C), 0)
    c = lax.broadcasted_iota(jnp.int32, (C, C), 1)
    incl = r >= c
    strict = r > c
    eye = r == c
    tril_bf = jnp.where(incl, 1.0, 0.0).astype(BF16)
    eye_f = jnp.where(eye, 1.0, 0.0)

    for c0 in range(0, tm, C):
        rows = slice(c0, c0 + C)
        gc_all = _dot_sel(tril_bf, g_all[rows])
        for h in range(n_heads):
            ls = slice(h * LANE, (h + 1) * LANE)
            q = qkv[rows, h * LANE:(h + 1) * LANE]
            k = qkv[rows, W + h * LANE:W + (h + 1) * LANE]
            v = qkv[rows, 2 * W + h * LANE:2 * W + (h + 1) * LANE]
            q = q * lax.rsqrt(jnp.sum(q * q, axis=-1, keepdims=True) + EPS) * (LANE ** -0.5)
            k = k * lax.rsqrt(jnp.sum(k * k, axis=-1, keepdims=True) + EPS)
            beta = beta_all[rows, n_heads + h:n_heads + h + 1]
            gc = gc_all[:, h:h + 1]
            gc_row = jnp.sum(jnp.where(eye, jnp.broadcast_to(gc, (C, C)), 0.0), axis=0, keepdims=True)
            decay = jnp.exp(jnp.where(incl, gc - gc_row, NEG))
            egc = jnp.exp(gc)
            g_last = gc[C - 1:C, :]
            kb = k * beta
            kbf = k.astype(BF16)
            L = jnp.where(strict, _dot_nt(kb.astype(BF16), kbf) * decay, 0.0)
            T = eye_f - L
            P = L
            for _ in range(5):
                P = _dot3(P, P)
                T = T + _dot3(T, P)
            Tb = T.astype(BF16)
            value = _dot(Tb, (v * beta).astype(BF16))
            kcd = _dot(Tb, (kb * egc).astype(BF16))
            qk = jnp.where(incl, _dot_nt(q.astype(BF16), kbf) * decay, 0.0)
            qg = q * egc
            kdec = k * jnp.exp(g_last - gc)
            St = state[h]
            Sb = St.astype(BF16)
            v_new = value - _dot(kcd.astype(BF16), Sb)
            vnb = v_new.astype(BF16)
            o = _dot(qg.astype(BF16), Sb) + _dot(qk.astype(BF16), vnb)
            state[h] = St * jnp.exp(g_last) + _dot_tn(kdec.astype(BF16), vnb)
            on = o * lax.rsqrt(jnp.mean(o * o, axis=-1, keepdims=True) + EPS) * og_ref[...]
            o_ref[rows, ls] = (on * _silu(gz_ref[rows, ls])).astype(BF16)

    @pl.when(i == pl.num_programs(1) - 1)
    def _():
        sfin_ref[0] = state[...]


def _delta_chunked(z, z_tail, cw, a_log, dtb, og, *, batch, seq, tm):
    H = a_log.shape[0]
    W = H * LANE
    nblk = seq // tm
    row = lambda b, i: b * nblk + i
    pad = lambda t: jnp.zeros((1, LANE), F32).at[0, :H].set(t)
    return pl.pallas_call(
        functools.partial(_delta_kernel, n_heads=H),
        grid=(batch, nblk),
        in_specs=[pl.BlockSpec((tm, W), lambda b, i: (row(b, i), 2)),
                  pl.BlockSpec((tm, W), lambda b, i: (row(b, i), 3)),
                  pl.BlockSpec((tm, W), lambda b, i: (row(b, i), 4)),
                  pl.BlockSpec((tm, LANE), lambda b, i: (row(b, i), 0)),
                  pl.BlockSpec((tm, W), lambda b, i: (row(b, i), 5)),
                  pl.BlockSpec((CONV_W, 3 * W), lambda b, i: (0, 0)),
                  pl.BlockSpec((1, LANE), lambda b, i: (0, 0)),
                  pl.BlockSpec((1, LANE), lambda b, i: (0, 0)),
                  pl.BlockSpec((1, LANE), lambda b, i: (0, 0))],
        out_specs=[pl.BlockSpec((tm, W), lambda b, i: (row(b, i), 0)),
                   pl.BlockSpec((1, H, LANE, LANE), lambda b, i: (b, 0, 0, 0))],
        out_shape=[jax.ShapeDtypeStruct((batch * seq, W), BF16),
                   jax.ShapeDtypeStruct((batch, H, LANE, LANE), F32)],
        scratch_shapes=[pltpu.VMEM((tm + 8, 3 * W), F32), pltpu.VMEM((H, LANE, LANE), F32)],
        compiler_params=_params(2),
        name="delta_chunked",
    )(z, z, z, z_tail, z, cw, pad(a_log), pad(dtb), og.reshape(1, LANE))


def _even_step_kernel(zu_ref, zv_ref, zq_ref, zk_ref, zvv_ref, gz_ref, zg_ref, cb_ref, s0_ref,
                      lng_ref, lnb_ref, w00_ref, b0_ref, cw_ref, alog_ref, dtb_ref, og_ref,
                      ya_ref, vrow_ref, ob_ref, cnew_ref, snew_ref, *, n_heads):
    nb = zu_ref.shape[0]
    W = n_heads * LANE
    u = _gelu(zu_ref[...])
    vg = _gelu(zv_ref[...])
    raw = jnp.concatenate([zq_ref[...], zk_ref[...], zvv_ref[...]], axis=1)
    cb = cb_ref[...]
    conv = raw * cw_ref[3:4, :]
    for s in range(CONV_W - 1):
        conv = conv + cb[:, s * 3 * W:(s + 1) * 3 * W] * cw_ref[s:s + 1, :]
    cnew_ref[:, 0:6 * W] = cb[:, 3 * W:9 * W]
    cnew_ref[:, 6 * W:9 * W] = raw
    qkv = _silu(conv)
    zg = zg_ref[...]
    g_all = -jnp.exp(alog_ref[...]) * jax.nn.softplus(zg + dtb_ref[...])
    beta_all = jax.nn.sigmoid(zg)
    a_all = jnp.exp(g_all)
    r = lax.broadcasted_iota(jnp.int32, (LANE, LANE), 0)
    c = lax.broadcasted_iota(jnp.int32, (LANE, LANE), 1)
    eye = r == c

    def col(row):
        return jnp.sum(jnp.where(eye, jnp.broadcast_to(row, (LANE, LANE)), 0.0), axis=1, keepdims=True)

    for h in range(n_heads):
        ls = slice(h * LANE, (h + 1) * LANE)
        vh = vg[:, ls]
        mu = jnp.mean(vh, axis=-1, keepdims=True)
        d = vh - mu
        var = jnp.mean(d * d, axis=-1, keepdims=True)
        vn = d * lax.rsqrt(var + EPS) * lng_ref[:, ls] + lnb_ref[:, ls]
        vrow_ref[:, ls] = vn
        ya_ref[:, ls] = (u[:, ls] * (vn * w00_ref[:, ls] + b0_ref[:, ls])).astype(BF16)
        q = qkv[:, h * LANE:(h + 1) * LANE]
        k = qkv[:, W + h * LANE:W + (h + 1) * LANE]
        v = qkv[:, 2 * W + h * LANE:2 * W + (h + 1) * LANE]
        q = q * lax.rsqrt(jnp.sum(q * q, axis=-1, keepdims=True) + EPS) * (LANE ** -0.5)
        k = k * lax.rsqrt(jnp.sum(k * k, axis=-1, keepdims=True) + EPS)
        for j in range(nb):
            kc = col(k[j:j + 1])
            qc = col(q[j:j + 1])
            Sd = s0_ref[j, h] * a_all[j:j + 1, h:h + 1]
            kS = jnp.sum(kc * Sd, axis=0, keepdims=True)
            delta = (v[j:j + 1] - kS) * beta_all[j:j + 1, n_heads + h:n_heads + h + 1]
            Sn = Sd + kc * delta
            snew_ref[j, h] = Sn
            o = jnp.sum(qc * Sn, axis=0, keepdims=True)
            on = o * lax.rsqrt(jnp.mean(o * o, axis=-1, keepdims=True) + EPS) * og_ref[...]
            ob_ref[j:j + 1, ls] = (on * _silu(gz_ref[j:j + 1, ls])).astype(BF16)


def _even_step(z, z_tail, conv_buf, s0, ln_g, ln_b, ws, bs, cw, a_log, dtb, og, *, nb):
    Bs = z.shape[0]
    H = a_log.shape[0]
    W = H * LANE
    pad = lambda t: jnp.zeros((1, LANE), F32).at[0, :H].set(t)
    w00 = jnp.repeat(ws[:, 0, 0], LANE).reshape(1, W)
    b0 = jnp.repeat(bs[:, 0], LANE).reshape(1, W)
    cb = conv_buf.reshape(Bs, (CONV_W - 1) * 3 * W)
    zspec = lambda j: pl.BlockSpec((nb, W), lambda i: (i, j))
    vec = lambda n: pl.BlockSpec((1, n), lambda i: (0, 0))
    ya, vrow, ob, cnew, snew = pl.pallas_call(
        functools.partial(_even_step_kernel, n_heads=H),
        grid=(Bs // nb,),
        in_specs=[zspec(0), zspec(1), zspec(2), zspec(3), zspec(4), zspec(5),
                  pl.BlockSpec((nb, LANE), lambda i: (i, 0)),
                  pl.BlockSpec((nb, 9 * W), lambda i: (i, 0)),
                  pl.BlockSpec((nb, H, LANE, LANE), lambda i: (i, 0, 0, 0)),
                  vec(W), vec(W), vec(W), vec(W),
                  pl.BlockSpec((CONV_W, 3 * W), lambda i: (0, 0)),
                  vec(LANE), vec(LANE), vec(LANE)],
        out_specs=[pl.BlockSpec((nb, W), lambda i: (i, 0)),
                   pl.BlockSpec((nb, W), lambda i: (i, 0)),
                   pl.BlockSpec((nb, W), lambda i: (i, 0)),
                   pl.BlockSpec((nb, 9 * W), lambda i: (i, 0)),
                   pl.BlockSpec((nb, H, LANE, LANE), lambda i: (i, 0, 0, 0))],
        out_shape=[jax.ShapeDtypeStruct((Bs, W), BF16),
                   jax.ShapeDtypeStruct((Bs, W), F32),
                   jax.ShapeDtypeStruct((Bs, W), BF16),
                   jax.ShapeDtypeStruct((Bs, 9 * W), F32),
                   jax.ShapeDtypeStruct((Bs, H, LANE, LANE), F32)],
        compiler_params=_params(1),
        name="even_step_sample",
    )(z, z, z, z, z, z, z_tail, cb, s0, ln_g.reshape(1, W), ln_b.reshape(1, W), w00, b0, cw,
      pad(a_log), pad(dtb), og.reshape(1, LANE))
    return ya, vrow, ob, cnew.reshape(Bs, CONV_W - 1, 3 * W), snew


def _logf_kernel(f_ref, bf_ref, lf_ref, cum_ref, carry):
    @pl.when(pl.program_id(1) == 0)
    def _():
        carry[...] = jnp.zeros(carry.shape, F32)

    tm = f_ref.shape[0]
    lf = jax.nn.log_sigmoid(f_ref[...] + bf_ref[...])
    lf_ref[...] = lf
    r = lax.broadcasted_iota(jnp.int32, (tm, tm), 0)
    c = lax.broadcasted_iota(jnp.int32, (tm, tm), 1)
    tril_bf = jnp.where(r >= c, 1.0, 0.0).astype(BF16)
    cum = _dot_sel(tril_bf, lf) + carry[...]
    cum_ref[...] = cum
    carry[...] = cum[tm - 1:tm, :]


def _logf_cumsum(z_tail, b_f_pad, *, batch, seq, tm):
    nblk = seq // tm
    return pl.pallas_call(
        _logf_kernel,
        grid=(batch, nblk),
        in_specs=[pl.BlockSpec((tm, LANE), lambda b, i: (b * nblk + i, 0)),
                  pl.BlockSpec((1, LANE), lambda b, i: (0, 0))],
        out_specs=[pl.BlockSpec((tm, LANE), lambda b, i: (b * nblk + i, 0)),
                   pl.BlockSpec((tm, LANE), lambda b, i: (b * nblk + i, 0))],
        out_shape=[jax.ShapeDtypeStruct((batch * seq, LANE), F32),
                   jax.ShapeDtypeStruct((batch * seq, LANE), F32)],
        scratch_shapes=[pltpu.VMEM((1, LANE), F32)],
        compiler_params=_params(2),
        name="logf_cumsum",
    )(z_tail, b_f_pad)


def _pair_norm(x, g, lo, dh):
    x2 = x * x
    sa = jnp.sum(jnp.where(lo, x2, 0.0), axis=-1, keepdims=True)
    sb = jnp.sum(jnp.where(lo, 0.0, x2), axis=-1, keepdims=True)
    ms = jnp.where(lo, sa, sb) * (1.0 / dh)
    return x * lax.rsqrt(ms + EPS) * g


def _fox_kernel(q_ref, k_ref, v_ref, f_ref, qg_ref, kg_ref, o_ref, kn_ref, kbf, vbf, *, tq, dh):
    i = pl.program_id(2)
    S = k_ref.shape[1]
    lo = lax.broadcasted_iota(jnp.int32, (1, LANE), 1) < dh

    @pl.when(i == 0)
    def _():
        for c0 in range(0, S, tq):
            kn = _pair_norm(k_ref[0, c0:c0 + tq, :], kg_ref[...], lo, dh)
            kn_ref[0, c0:c0 + tq, :] = kn
            kbf[c0:c0 + tq, :] = kn.astype(BF16)
            vbf[c0:c0 + tq, :] = v_ref[0, c0:c0 + tq, :].astype(BF16)

    q = _pair_norm(q_ref[0], qg_ref[...], lo, dh) * (dh ** -0.5)
    qs = (jnp.where(lo, q, 0.0).astype(BF16), jnp.where(lo, 0.0, q).astype(BF16))
    rr = lax.broadcasted_iota(jnp.int32, (tq, tq), 0)
    cc = lax.broadcasted_iota(jnp.int32, (tq, tq), 1)
    causal = cc <= rr

    def chunk(j, carry, masked):
        off = pl.multiple_of(j * tq, tq)
        kj = kbf[pl.ds(off, tq), :]
        vj = vbf[pl.ds(off, tq), :]
        out = []
        for hh in range(2):
            m, l, acc = carry[hh]
            fj = f_ref[0, 0, hh, pl.ds(j, 1), :]
            s = _dot_nt(qs[hh], kj) - fj
            if masked:
                s = jnp.where(causal, s, NEG)
            m_new = jnp.maximum(m, jnp.max(s, axis=-1, keepdims=True))
            alpha = jnp.exp(m - m_new)
            p = jnp.exp(s - m_new)
            l = alpha * l + jnp.sum(p, axis=-1, keepdims=True)
            acc = alpha * acc + _dot(p.astype(BF16), vj)
            out.append((m_new, l, acc))
        return tuple(out)

    init = tuple((jnp.full((tq, 1), NEG, F32), jnp.zeros((tq, 1), F32), jnp.zeros((tq, LANE), F32))
                 for _ in range(2))
    carry = lax.fori_loop(0, i, lambda j, cr: chunk(j, cr, False), init)
    (ma, la, acca), (mb, lb, accb) = chunk(i, carry, True)
    o_ref[0] = jnp.where(lo, acca / la, accb / lb).astype(BF16)


def _fox_prompt(z3, fcum, qg, kg, *, tq):
    B, S, N3 = z3.shape
    dh = qg.shape[0]
    H = N3 // (3 * dh)
    HP = H // 2
    nq = S // tq
    f5 = fcum.reshape(B, HP, 2, nq, tq)
    g2 = lambda g: jnp.concatenate([g, g]).reshape(1, LANE)
    return pl.pallas_call(
        functools.partial(_fox_kernel, tq=tq, dh=dh),
        grid=(B, HP, nq),
        in_specs=[pl.BlockSpec((1, tq, LANE), lambda b, p, i: (b, i, p)),
                  pl.BlockSpec((1, S, LANE), lambda b, p, i: (b, 0, HP + p)),
                  pl.BlockSpec((1, S, LANE), lambda b, p, i: (b, 0, 2 * HP + p)),
                  pl.BlockSpec((1, 1, 2, nq, tq), lambda b, p, i: (b, p, 0, 0, 0)),
                  pl.BlockSpec((1, LANE), lambda b, p, i: (0, 0)),
                  pl.BlockSpec((1, LANE), lambda b, p, i: (0, 0))],
        out_specs=[pl.BlockSpec((1, tq, LANE), lambda b, p, i: (b, i, p)),
                   pl.BlockSpec((1, S, LANE), lambda b, p, i: (b, 0, p))],
        out_shape=[jax.ShapeDtypeStruct((B, S, H * dh), BF16),
                   jax.ShapeDtypeStruct((B, S, H * dh), F32)],
        scratch_shapes=[pltpu.VMEM((S, LANE), BF16), pltpu.VMEM((S, LANE), BF16)],
        compiler_params=_params(3),
        name="fox_attention_prompt",
    )(z3, z3, z3, f5, g2(qg), g2(kg))


def _head_sel(n_heads, dh):
    W = n_heads * dh
    col_head = jnp.arange(W, dtype=jnp.int32) // dh
    G = (col_head[:, None] == jnp.arange(LANE, dtype=jnp.int32)[None, :]).astype(BF16)
    return G, G.T


def _qknorm_kernel(q_ref, k_ref, f_ref, qg_ref, kg_ref, bf_ref, g_ref, e_ref, qo_ref, ko_ref, lf_ref, *, dh):
    G = g_ref[...]
    E = e_ref[...]

    def norm(x, gain):
        ms = _x_dot_sel(x * x, G) * (1.0 / dh)
        return x * _x_dot_sel(lax.rsqrt(ms + EPS), E) * gain

    qo_ref[...] = norm(q_ref[...], qg_ref[...]) * (dh ** -0.5)
    ko_ref[...] = norm(k_ref[...], kg_ref[...])
    lf_ref[...] = jax.nn.log_sigmoid(f_ref[...] + bf_ref[...])


def _qknorm_sample(z, z_tail, qg, kg, b_f_pad, G, E, *, n_heads):
    Bs = z.shape[0]
    dh = qg.shape[0]
    W = n_heads * dh
    tile = lambda g: jnp.tile(g, n_heads).reshape(1, W)
    full = lambda a: pl.BlockSpec(a.shape, lambda i: (0,) * a.ndim)
    return pl.pallas_call(
        functools.partial(_qknorm_kernel, dh=dh),
        grid=(1,),
        in_specs=[pl.BlockSpec((Bs, W), lambda i: (0, 0)),
                  pl.BlockSpec((Bs, W), lambda i: (0, 1)),
                  full(z_tail), pl.BlockSpec((1, W), lambda i: (0, 0)), pl.BlockSpec((1, W), lambda i: (0, 0)),
                  full(b_f_pad), full(G), full(E)],
        out_specs=[pl.BlockSpec((Bs, W), lambda i: (0, 0)),
                   pl.BlockSpec((Bs, W), lambda i: (0, 0)),
                   pl.BlockSpec((Bs, LANE), lambda i: (0, 0))],
        out_shape=[jax.ShapeDtypeStruct((Bs, W), F32), jax.ShapeDtypeStruct((Bs, W), F32),
                   jax.ShapeDtypeStruct((Bs, LANE), F32)],
        compiler_params=_params(1),
        name="qknorm_sample",
    )(z, z, z_tail, tile(qg), tile(kg), b_f_pad, G, E)


def _decode_kernel(pt_ref, q_ref, kn_ref, vn_ref, lfn_ref, g_ref, e_ref, *rest, pages, n_heads):
    k_refs = rest[:pages]
    v_refs = rest[pages:2 * pages]
    lf_refs = rest[2 * pages:3 * pages]
    o_ref = rest[3 * pages]
    m_scr, l_scr, acc_scr, c_scr = rest[3 * pages + 1:]
    j = pl.program_id(1)
    H = n_heads
    G = g_ref[...]
    E = e_ref[0:H, :]
    q = q_ref[0]

    @pl.when(j == 0)
    def _():
        m_scr[...] = jnp.full(m_scr.shape, NEG, F32)
        l_scr[...] = jnp.zeros(l_scr.shape, F32)
        acc_scr[...] = jnp.zeros(acc_scr.shape, F32)
        c_scr[...] = jnp.zeros(c_scr.shape, F32)

    def spread(x):
        return _x_dot_sel(jnp.broadcast_to(x, (8, H)), E)

    def update(s2, vals, m, l, acc, single):
        m_new = jnp.maximum(m, jnp.max(s2, axis=0, keepdims=True))
        alpha = jnp.exp(m - m_new)
        p = jnp.exp(s2 - m_new)
        l = alpha * l + jnp.sum(p, axis=0, keepdims=True)
        if single:
            pv = _dot(jnp.broadcast_to(p, (8, H)).astype(BF16), E) * vals
            sub = lax.broadcasted_iota(jnp.int32, (8, 1), 0)
            contrib = jnp.where(sub == 0, pv, 0.0)
        else:
            pv = _dot(p.astype(BF16), E) * vals
            R = pv.shape[0]
            contrib = jnp.sum(pv.reshape(R // 8, 8, pv.shape[1]), axis=0)
        acc = acc * spread(alpha) + contrib
        return m_new, l, acc

    P = k_refs[0].shape[2]
    r = lax.broadcasted_iota(jnp.int32, (P, P), 0)
    c = lax.broadcasted_iota(jnp.int32, (P, P), 1)
    tril_bf = jnp.where(r >= c, 1.0, 0.0).astype(BF16)

    m, l, acc, carry = m_scr[...], l_scr[...], acc_scr[...], c_scr[...]
    for p in range(pages):
        prod = k_refs[p][0, 0] * q
        ph, plo = _split2(prod)
        s = (_dot(ph, G) + _dot(plo, G))[:, 0:H]
        fcum = _dot_sel(tril_bf, lf_refs[p][0, 0]) + carry
        carry = fcum[P - 1:P, :]
        m, l, acc = update(s - fcum, v_refs[p][0, 0], m, l, acc, False)
    m_scr[...], l_scr[...], acc_scr[...], c_scr[...] = m, l, acc, carry

    @pl.when(j == pl.num_programs(1) - 1)
    def _():
        prod = jnp.broadcast_to(kn_ref[0] * q, (8, q.shape[1]))
        ph, plo = _split2(prod)
        s = (_dot(ph, G) + _dot(plo, G))[0:1, 0:H]
        fnew = carry + lfn_ref[0][:, 0:H]
        m2, l2, acc2 = update(s - fnew, jnp.broadcast_to(vn_ref[0], (8, q.shape[1])), m, l, acc, True)
        o = jnp.sum(acc2, axis=0, keepdims=True) / spread(l2)[0:1, :]
        o_ref[0] = o.astype(BF16)


def _decode_attention(page_table, q, kn, vn, lfn, cache_k, cache_v, cache_lf, G, E, *, layer, n_heads):
    Bs, W = q.shape
    npg = page_table.shape[1]
    P = cache_k.shape[2]
    pp = PAGES_PER_STEP
    pt = page_table.reshape(-1)

    def page(p):
        return lambda b, j, pt_ref: (layer, pt_ref[b * npg + j * pp + p], 0, 0)

    row = lambda n: pl.BlockSpec((1, 1, n), lambda b, j, pt_ref: (b, 0, 0))
    in_specs = [row(W), row(W), row(W), row(LANE),
                pl.BlockSpec(G.shape, lambda b, j, pt_ref: (0, 0)),
                pl.BlockSpec(E.shape, lambda b, j, pt_ref: (0, 0))]
    in_specs += [pl.BlockSpec((1, 1, P, W), page(p)) for p in range(pp)]
    in_specs += [pl.BlockSpec((1, 1, P, W), page(p)) for p in range(pp)]
    in_specs += [pl.BlockSpec((1, 1, P, n_heads), page(p)) for p in range(pp)]
    r3 = lambda a: a.reshape(Bs, 1, a.shape[1])
    out = pl.pallas_call(
        functools.partial(_decode_kernel, pages=pp, n_heads=n_heads),
        grid_spec=pltpu.PrefetchScalarGridSpec(
            num_scalar_prefetch=1,
            grid=(Bs, npg // pp),
            in_specs=in_specs,
            out_specs=pl.BlockSpec((1, 1, W), lambda b, j, pt_ref: (b, 0, 0)),
            scratch_shapes=[pltpu.VMEM((1, n_heads), F32), pltpu.VMEM((1, n_heads), F32),
                            pltpu.VMEM((8, W), F32), pltpu.VMEM((1, n_heads), F32)]),
        out_shape=jax.ShapeDtypeStruct((Bs, 1, W), BF16),
        compiler_params=_params(2),
        name="fox_decode_paged",
    )(pt, r3(q), r3(kn), r3(vn), r3(lfn), G, E,
      *([cache_k] * pp), *([cache_v] * pp), *([cache_lf] * pp))
    return out.reshape(Bs, W)


def _router_kernel(x_ref, g_ref, sc_ref, sh_ref, wr_ref, br_ref, h_ref, lg_ref):
    h = _rms_mod(x_ref[...], g_ref[...], sc_ref[0], sh_ref[0])
    h_ref[...] = h.astype(BF16)
    lg_ref[...] = _dot3(h, wr_ref[...]) + br_ref[...]


def _router(x, g, sc, sh, wr_pad, br_pad, *, tm, rows_per_mod):
    M, D = x.shape
    bpm = rows_per_mod // tm
    R = sc.shape[1]
    mod = pl.BlockSpec((1, R, D), lambda i: (i // bpm, 0, 0))
    return pl.pallas_call(
        _router_kernel,
        grid=(M // tm,),
        in_specs=[pl.BlockSpec((tm, D), lambda i: (i, 0)),
                  pl.BlockSpec((1, D), lambda i: (0, 0)),
                  mod, mod,
                  pl.BlockSpec((D, LANE), lambda i: (0, 0)),
                  pl.BlockSpec((1, LANE), lambda i: (0, 0))],
        out_specs=[pl.BlockSpec((tm, D), lambda i: (i, 0)),
                   pl.BlockSpec((tm, LANE), lambda i: (i, 0))],
        out_shape=[jax.ShapeDtypeStruct((M, D), BF16), jax.ShapeDtypeStruct((M, LANE), F32)],
        compiler_params=_params(1),
        name="router",
    )(x, g, sc, sh, wr_pad, br_pad)


def _moe_kernel(be_ref, na_ref, x_ref, w1_ref, b1_ref, w2_ref, b2_ref, o_ref, w1bf, w2bf, *, f_chunk):
    i = pl.program_id(0)
    active = i < na_ref[0]
    e = be_ref[i]
    prev = be_ref[jnp.maximum(i - 1, 0)]
    first = jnp.logical_or(i == 0, e != prev)
    FF = w2bf.shape[0]

    @pl.when(jnp.logical_and(active, first))
    def _():
        n = w1bf.shape[0] // 128

        def body(r, c):
            sl = pl.ds(pl.multiple_of(r * 128, 128), 128)
            w1bf[sl, :] = w1_ref[0, sl, :].astype(BF16)
            return c

        lax.fori_loop(0, n, body, 0)
        n2 = FF // 128

        def body2(r, c):
            sl = pl.ds(pl.multiple_of(r * 128, 128), 128)
            w2bf[sl, :] = w2_ref[0, sl, :].astype(BF16)
            return c

        lax.fori_loop(0, n2, body2, 0)

    @pl.when(active)
    def _():
        x = x_ref[...]
        acc = None
        for f0 in range(0, FF, f_chunk):
            glu = _dot(x, w1bf[:, f0:f0 + f_chunk]) + b1_ref[0, :, f0:f0 + f_chunk]
            lin = _dot(x, w1bf[:, FF + f0:FF + f0 + f_chunk]) + b1_ref[0, :, FF + f0:FF + f0 + f_chunk]
            glu = jnp.minimum(glu, SWIGLU_LIMIT)
            lin = jnp.clip(lin, -SWIGLU_LIMIT, SWIGLU_LIMIT)
            a = glu * jax.nn.sigmoid(SWIGLU_ALPHA * glu) * (lin + 1.0)
            t = _dot(a.astype(BF16), w2bf[f0:f0 + f_chunk, :])
            acc = t if acc is None else acc + t
        o_ref[...] = acc + b2_ref[0]

    @pl.when(jnp.logical_not(active))
    def _():
        o_ref[...] = jnp.zeros(o_ref.shape, F32)


def _moe_experts(blk_e, n_active, xb, w1, b1, w2, b2, *, bm):
    NS, D = xb.shape
    E, _, F2 = w1.shape
    FF = F2 // 2
    nb = NS // bm
    return pl.pallas_call(
        functools.partial(_moe_kernel, f_chunk=512),
        grid_spec=pltpu.PrefetchScalarGridSpec(
            num_scalar_prefetch=2,
            grid=(nb,),
            in_specs=[pl.BlockSpec((bm, D), lambda i, be, na: (i, 0)),
                      pl.BlockSpec((1, D, F2), lambda i, be, na: (be[i], 0, 0)),
                      pl.BlockSpec((1, 1, F2), lambda i, be, na: (be[i], 0, 0)),
                      pl.BlockSpec((1, FF, D), lambda i, be, na: (be[i], 0, 0)),
                      pl.BlockSpec((1, 1, D), lambda i, be, na: (be[i], 0, 0))],
            out_specs=pl.BlockSpec((bm, D), lambda i, be, na: (i, 0)),
            scratch_shapes=[pltpu.VMEM((D, F2), BF16), pltpu.VMEM((FF, D), BF16)]),
        out_shape=jax.ShapeDtypeStruct((NS, D), F32),
        compiler_params=_params(1),
        name="moe_experts",
    )(blk_e, n_active, xb, w1, b1.reshape(E, 1, F2), w2, b2.reshape(E, 1, D))


def _moe_ffn(h_all, logits, w1, b1, w2, b2):
    T, D = h_all.shape
    bm = MOE_TILE
    top_v, top_i = lax.top_k(logits, TOP_K)
    gate = jax.nn.softmax(top_v, axis=-1)
    n = T * TOP_K
    flat_e = top_i.reshape(-1).astype(jnp.int32)
    flat_t = jnp.repeat(jnp.arange(T, dtype=jnp.int32), TOP_K)
    onehot = (flat_e[:, None] == jnp.arange(N_EXPERTS, dtype=jnp.int32)[None, :]).astype(jnp.int32)
    csum = jnp.cumsum(onehot, axis=0)
    rank = jnp.take_along_axis(csum, flat_e[:, None], axis=1)[:, 0] - 1
    counts = csum[-1]
    nblk_e = (counts + bm - 1) // bm
    blk_end = jnp.cumsum(nblk_e)
    blk_start = blk_end - nblk_e
    dest = blk_start[flat_e] * bm + rank
    nb = -(-(n + N_EXPERTS * (bm - 1)) // bm)
    n_active = blk_end[-1]
    bidx = jnp.minimum(jnp.arange(nb, dtype=jnp.int32), n_active - 1)
    blk_e = jnp.minimum(jnp.searchsorted(blk_end, bidx, side='right'), N_EXPERTS - 1).astype(jnp.int32)
    slot_tok = jnp.zeros((nb * bm,), jnp.int32).at[dest].set(flat_t)
    xb = jnp.take(h_all, slot_tok, axis=0)
    yb = _moe_experts(blk_e, n_active.reshape(1).astype(jnp.int32), xb, w1, b1, w2, b2, bm=bm)
    ya = jnp.take(yb, dest, axis=0).reshape(T, TOP_K, D)
    return jnp.sum(ya * gate[:, :, None], axis=1)


def kernel(x_prompt, x_sample, cache_k, cache_v, cache_logf, state_delta, state_conv, page_table, c_prompt, c_sample, norm_g, w_ada, b_ada, w_in_even, a_ln_g, a_ln_b, a_ws, a_bs, conv_w, A_log, dt_bias, onorm_g, w_out_even, w_in_odd, b_f, q_norm_g, k_norm_g, w_out_odd, w_router, b_router, w_mlp1, b_mlp1, w_mlp2, b_mlp2):
    Bp, Sp, D = x_prompt.shape
    Bs = x_sample.shape[0]
    depth = norm_g.shape[0]
    Mp = Bp * Sp
    a_heads = a_ws.shape[1]
    a_width = a_heads * LANE
    b_heads = A_log.shape[1]
    b_width = b_heads * LANE
    c_heads = b_f.shape[1]
    c_dh = q_norm_g.shape[1]
    c_width = c_heads * c_dh
    n_pool, page = cache_k.shape[1], cache_k.shape[2]
    tm = ROW_TILE

    xp = x_prompt.reshape(Mp, D)
    xs = x_sample.reshape(Bs, D)
    mods = _ada_all(jnp.concatenate([c_prompt, c_sample], axis=0), w_ada, b_ada)
    ck = cache_k.reshape(cache_k.shape[0], n_pool, page, c_width)
    cv = cache_v.reshape(cache_v.shape[0], n_pool, page, c_width)
    G, E = _head_sel(c_heads, c_dh)

    def pad_cols(w):
        return jnp.pad(w, ((0, 0), (0, LANE - w.shape[1])))

    kp_l, vp_l, fp_l, ks_l, vs_l, fs_l = [], [], [], [], [], []
    dp_l, cp_l, ds_l, cs_l, av_l = [], [], [], [], []

    for l in range(depth):
        i = l // 2
        m6 = mods[l].reshape(Bp + Bs, 6, D)
        mp = [m6[:Bp, j].reshape(Bp, 1, D) for j in range(6)]
        ms = [m6[Bp:, j].reshape(1, Bs, D) for j in range(6)]
        g1 = norm_g[l, 0].reshape(1, D)
        g2 = norm_g[l, 1].reshape(1, D)
        if l % 2 == 0:
            n_main = 2 * a_width + 4 * b_width
            w_main = w_in_even[i][:, :n_main]
            w_tail = pad_cols(w_in_even[i][:, n_main:])
            zp, ztp = _proj(xp, g1, mp[1], mp[0], w_main, w_tail, tm=tm, rows_per_mod=Sp)
            zs, zts = _proj(xs, g1, ms[1], ms[0], w_main, w_tail, tm=Bs, rows_per_mod=Bs)
            ya_p = _gmlp(zp, a_ln_g[i], a_ln_b[i], a_ws[i], a_bs[i], tm=2 * CHUNK)
            ob_p, sfin_p = _delta_chunked(zp, ztp, conv_w[i], A_log[i], dt_bias[i], onorm_g[i],
                                          batch=Bp, seq=Sp, tm=2 * DN_CHUNK)
            ya_s, vrow_s, ob_s, cnew_s, snew_s = _even_step(
                zs, zts, state_conv[i], state_delta[i], a_ln_g[i], a_ln_b[i], a_ws[i], a_bs[i],
                conv_w[i], A_log[i], dt_bias[i], onorm_g[i], nb=8)
            xp = _outproj([ya_p, ob_p], w_out_even[i], xp, mp[2], tm=tm, rows_per_mod=Sp)
            xs = _outproj([ya_s, ob_s], w_out_even[i], xs, ms[2], tm=Bs, rows_per_mod=Bs)
            qkv_raw = zp[:, 2 * a_width:2 * a_width + 3 * b_width].reshape(Bp, Sp, 3 * b_width)
            dp_l.append(sfin_p)
            cp_l.append(qkv_raw[:, Sp - (CONV_W - 1):])
            ds_l.append(snew_s)
            cs_l.append(cnew_s)
            av_l.append(vrow_s.reshape(Bs, 1, a_width))
        else:
            n_main = 3 * c_width
            w_main = w_in_odd[i][:, :n_main]
            w_tail = pad_cols(w_in_odd[i][:, n_main:])
            bf_pad = jnp.zeros((1, LANE), F32).at[0, :c_heads].set(b_f[i])
            zp, ztp = _proj(xp, g1, mp[1], mp[0], w_main, w_tail, tm=tm, rows_per_mod=Sp)
            zs, zts = _proj(xs, g1, ms[1], ms[0], w_main, w_tail, tm=Bs, rows_per_mod=Bs)
            lf_p, fc_p = _logf_cumsum(ztp, bf_pad, batch=Bp, seq=Sp, tm=512)
            lf_p = lf_p[:, :c_heads].reshape(Bp, Sp, c_heads)
            fcum = jnp.swapaxes(fc_p[:, :c_heads].reshape(Bp, Sp, c_heads), 1, 2)
            o_p, kn_p = _fox_prompt(zp.reshape(Bp, Sp, n_main), fcum, q_norm_g[i], k_norm_g[i], tq=512)
            xp = _outproj([o_p.reshape(Mp, c_width)], w_out_odd[i], xp, mp[2], tm=tm, rows_per_mod=Sp)
            q_s, k_s, lf_s = _qknorm_sample(zs, zts, q_norm_g[i], k_norm_g[i], bf_pad, G, E, n_heads=c_heads)
            v_s = zs[:, 2 * c_width:]
            o_s = _decode_attention(page_table, q_s, k_s, v_s, lf_s, ck, cv, cache_logf, G, E,
                                    layer=i, n_heads=c_heads)
            xs = _outproj([o_s], w_out_odd[i], xs, ms[2], tm=Bs, rows_per_mod=Bs)
            kp_l.append(kn_p.reshape(Bp, Sp, c_heads, c_dh))
            vp_l.append(zp[:, 2 * c_width:].reshape(Bp, Sp, c_heads, c_dh))
            fp_l.append(lf_p)
            ks_l.append(k_s.reshape(Bs, 1, c_heads, c_dh))
            vs_l.append(v_s.reshape(Bs, 1, c_heads, c_dh))
            fs_l.append(lf_s[:, :c_heads].reshape(Bs, 1, c_heads))
        wr_pad = pad_cols(w_router[l])
        br_pad = jnp.zeros((1, LANE), F32).at[0, :N_EXPERTS].set(b_router[l])
        hp, lgp = _router(xp, g2, mp[4], mp[3], wr_pad, br_pad, tm=tm, rows_per_mod=Sp)
        hs, lgs = _router(xs, g2, ms[4], ms[3], wr_pad, br_pad, tm=Bs, rows_per_mod=Bs)
        h_all = jnp.concatenate([hp, hs], axis=0)
        lg_all = jnp.concatenate([lgp, lgs], axis=0)[:, :N_EXPERTS]
        y_all = _moe_ffn(h_all, lg_all, w_mlp1[l], b_mlp1[l], w_mlp2[l], b_mlp2[l])
        xp = xp + (mp[5] * y_all[:Mp].reshape(Bp, Sp, D)).reshape(Mp, D)
        xs = xs + ms[5][0] * y_all[Mp:]

    return (xp.reshape(Bp, Sp, D), xs.reshape(Bs, 1, D),
            jnp.stack(kp_l), jnp.stack(vp_l), jnp.stack(fp_l),
            jnp.stack(ks_l), jnp.stack(vs_l), jnp.stack(fs_l),
            jnp.stack(dp_l), jnp.stack(cp_l), jnp.stack(ds_l), jnp.stack(cs_l), jnp.stack(av_l))
```

```python
import functools

import jax
import jax.numpy as jnp
from jax import lax
from jax.experimental import pallas as pl
from jax.experimental.pallas import tpu as pltpu

F32 = jnp.float32
BF16 = jnp.bfloat16

EPS = 1e-6
CHUNK = 128
DN_CHUNK = 64
N_EXPERTS = 32
TOP_K = 4
SWIGLU_ALPHA = 1.702
SWIGLU_LIMIT = 7.0
CONV_W = 4
LANE = 128
NEG = -1e30

VMEM_LIMIT = 56 * 1024 * 1024
ROW_TILE = 512
MOE_TILE = 256
PAGES_PER_STEP = 8
PRECISE_MIXER_LAYERS = 2
PRECISE_MOE_LAYERS = 1


def _params(n_axes):
    return pltpu.CompilerParams(dimension_semantics=("arbitrary",) * n_axes,
                                vmem_limit_bytes=VMEM_LIMIT)


def _dot(a, b):
    return jnp.dot(a, b, preferred_element_type=F32)


def _dot_nt(a, b):
    return lax.dot_general(a, b, (((1,), (1,)), ((), ())), preferred_element_type=F32)


def _dot_tn(a, b):
    return lax.dot_general(a, b, (((0,), (0,)), ((), ())), preferred_element_type=F32)


def _split2(a):
    hi = a.astype(BF16)
    lo = (a - hi.astype(F32)).astype(BF16)
    return hi, lo


def _split3(a):
    hi = a.astype(BF16)
    r = a - hi.astype(F32)
    mid = r.astype(BF16)
    lo = (r - mid.astype(F32)).astype(BF16)
    return hi, mid, lo


def _dot3(a, b):
    ah, al = _split2(a)
    bh, bl = _split2(b)
    return _dot(ah, bh) + (_dot(ah, bl) + _dot(al, bh))


def _mm(a, b, precise, dot=_dot):
    if precise:
        ah, al = _split2(a)
        bh, bl = _split2(b)
        return dot(ah, bh) + (dot(ah, bl) + dot(al, bh))
    return dot(a.astype(BF16), b.astype(BF16))


def _mm_w(a, w_hi, w_lo):
    ah = a.astype(BF16)
    out = _dot(ah, w_hi)
    if w_lo is not None:
        al = (a - ah.astype(F32)).astype(BF16)
        out = out + (_dot(ah, w_lo) + _dot(al, w_hi))
    return out


def _dot_sel(sel_bf16, x):
    hi, mid, lo = _split3(x)
    return _dot(sel_bf16, hi) + (_dot(sel_bf16, mid) + _dot(sel_bf16, lo))


def _x_dot_sel(x, sel_bf16):
    hi, mid, lo = _split3(x)
    return _dot(hi, sel_bf16) + (_dot(mid, sel_bf16) + _dot(lo, sel_bf16))


def _rms_mod(x, g, sc, sh):
    ms = jnp.mean(x * x, axis=-1, keepdims=True)
    return (x * lax.rsqrt(ms + EPS) * g) * (1.0 + sc) + sh


def _gelu(x):
    return 0.5 * x * (1.0 + lax.erf(x * 0.7071067811865476))


def _silu(x):
    return x * jax.nn.sigmoid(x)


def _cast_rows(load, hi_ref, lo_ref, rows=128):
    n = hi_ref.shape[0] // rows

    def body(r, c):
        sl = pl.ds(pl.multiple_of(r * rows, rows), rows)
        w = load(sl)
        hi = w.astype(BF16)
        hi_ref[sl, :] = hi
        if lo_ref is not None:
            lo_ref[sl, :] = (w - hi.astype(F32)).astype(BF16)
        return c

    lax.fori_loop(0, n, body, 0)


def _ada_kernel(c_ref, w_ref, b_ref, o_ref):
    c = c_ref[...]
    o_ref[0] = _dot3(_silu(c), w_ref[0]) + b_ref[0]


def _ada_all(c_all, w_ada, b_ada):
    L, D, N = w_ada.shape
    nb = c_all.shape[0]
    tn = 1536
    return pl.pallas_call(
        _ada_kernel,
        grid=(L, N // tn),
        in_specs=[pl.BlockSpec((nb, D), lambda l, j: (0, 0)),
                  pl.BlockSpec((1, D, tn), lambda l, j: (l, 0, j)),
                  pl.BlockSpec((1, 1, tn), lambda l, j: (l, 0, j))],
        out_specs=pl.BlockSpec((1, nb, tn), lambda l, j: (l, 0, j)),
        out_shape=jax.ShapeDtypeStruct((L, nb, N), F32),
        compiler_params=_params(2),
        name="ada_mod",
    )(c_all, w_ada, b_ada.reshape(L, 1, N))


def _proj_kernel(x_ref, g_ref, sc_ref, sh_ref, w_ref, wt_ref, o_ref, ot_ref, *scr, n_chunk, precise):
    whi, wthi = scr[0], scr[1]
    wlo, wtlo = (scr[2], scr[3]) if precise else (None, None)

    @pl.when(pl.program_id(0) == 0)
    def _():
        _cast_rows(lambda sl: w_ref[sl, :], whi, wlo)
        _cast_rows(lambda sl: wt_ref[sl, :], wthi, wtlo)

    h = _rms_mod(x_ref[...], g_ref[...], sc_ref[0], sh_ref[0])
    hh = h.astype(BF16)
    hl = (h - hh.astype(F32)).astype(BF16) if precise else None

    def mm(w_hi, w_lo, cols):
        out = _dot(hh, w_hi[:, cols])
        if precise:
            out = out + (_dot(hh, w_lo[:, cols]) + _dot(hl, w_hi[:, cols]))
        return out

    for n0 in range(0, o_ref.shape[1], n_chunk):
        o_ref[:, n0:n0 + n_chunk] = mm(whi, wlo, slice(n0, n0 + n_chunk))
    ot_ref[...] = mm(wthi, wtlo, slice(0, ot_ref.shape[1]))


def _proj(x, g, sc, sh, w, w_tail, *, tm, rows_per_mod, precise):
    M, D = x.shape
    N = w.shape[1]
    NT = w_tail.shape[1]
    bpm = rows_per_mod // tm
    R = sc.shape[1]
    mod = pl.BlockSpec((1, R, D), lambda i: (i // bpm, 0, 0))
    parts = 2 if precise else 1
    return pl.pallas_call(
        functools.partial(_proj_kernel, n_chunk=512, precise=precise),
        grid=(M // tm,),
        in_specs=[pl.BlockSpec((tm, D), lambda i: (i, 0)),
                  pl.BlockSpec((1, D), lambda i: (0, 0)),
                  mod, mod,
                  pl.BlockSpec((D, N), lambda i: (0, 0), pipeline_mode=pl.Buffered(1)),
                  pl.BlockSpec((D, NT), lambda i: (0, 0), pipeline_mode=pl.Buffered(1))],
        out_specs=[pl.BlockSpec((tm, N), lambda i: (i, 0)),
                   pl.BlockSpec((tm, NT), lambda i: (i, 0))],
        out_shape=[jax.ShapeDtypeStruct((M, N), F32), jax.ShapeDtypeStruct((M, NT), F32)],
        scratch_shapes=[pltpu.VMEM((D, N), BF16), pltpu.VMEM((D, NT), BF16)] * parts,
        compiler_params=_params(1),
        name="norm_mod_proj",
    )(x, g, sc, sh, w, w_tail)


def _outproj_kernel(*refs, n_in, precise):
    ys = refs[:n_in]
    w_ref, x_ref, gate_ref, o_ref, whi = refs[n_in:n_in + 5]
    wlo = refs[n_in + 5] if precise else None

    @pl.when(pl.program_id(0) == 0)
    def _():
        _cast_rows(lambda sl: w_ref[sl, :], whi, wlo)

    acc = None
    k0 = 0
    for y_ref in ys:
        kk = y_ref.shape[1]
        t = _mm_w(y_ref[...], whi[k0:k0 + kk, :], wlo[k0:k0 + kk, :] if precise else None)
        acc = t if acc is None else acc + t
        k0 += kk
    o_ref[...] = x_ref[...] + gate_ref[0] * acc


def _outproj(ys, w, x, gate, *, tm, rows_per_mod, precise):
    M, D = x.shape
    K = w.shape[0]
    bpm = rows_per_mod // tm
    R = gate.shape[1]
    in_specs = [pl.BlockSpec((tm, y.shape[1]), lambda i: (i, 0)) for y in ys]
    in_specs += [pl.BlockSpec((K, D), lambda i: (0, 0), pipeline_mode=pl.Buffered(1)),
                 pl.BlockSpec((tm, D), lambda i: (i, 0)),
                 pl.BlockSpec((1, R, D), lambda i: (i // bpm, 0, 0))]
    return pl.pallas_call(
        functools.partial(_outproj_kernel, n_in=len(ys), precise=precise),
        grid=(M // tm,),
        in_specs=in_specs,
        out_specs=pl.BlockSpec((tm, D), lambda i: (i, 0)),
        out_shape=jax.ShapeDtypeStruct((M, D), F32),
        scratch_shapes=[pltpu.VMEM((K, D), BF16)] * (2 if precise else 1),
        compiler_params=_params(1),
        name="out_proj_residual",
    )(*ys, w, x, gate)


def _gmlp_kernel(u_ref, v_ref, lng_ref, lnb_ref, ws_ref, bst_ref, o_ref, *, n_heads, precise):
    tm = u_ref.shape[0]
    u = _gelu(u_ref[...])
    v = _gelu(v_ref[...])
    r = lax.broadcasted_iota(jnp.int32, (CHUNK, CHUNK), 0)
    c = lax.broadcasted_iota(jnp.int32, (CHUNK, CHUNK), 1)
    tril = r >= c
    for h in range(n_heads):
        ls = slice(h * LANE, (h + 1) * LANE)
        vh = v[:, ls]
        mu = jnp.mean(vh, axis=-1, keepdims=True)
        d = vh - mu
        var = jnp.mean(d * d, axis=-1, keepdims=True)
        vn = d * lax.rsqrt(var + EPS) * lng_ref[:, ls] + lnb_ref[:, ls]
        wm = jnp.where(tril, ws_ref[h], 0.0)
        bias = bst_ref[:, h:h + 1]
        for c0 in range(0, tm, CHUNK):
            mixed = _mm(wm, vn[c0:c0 + CHUNK], precise) + bias
            o_ref[c0:c0 + CHUNK, ls] = (u[c0:c0 + CHUNK, ls] * mixed).astype(o_ref.dtype)


def _gmlp(z, ln_g, ln_b, ws, bs, *, tm, precise):
    M = z.shape[0]
    H = ws.shape[0]
    W = H * LANE
    return pl.pallas_call(
        functools.partial(_gmlp_kernel, n_heads=H, precise=precise),
        grid=(M // tm,),
        in_specs=[pl.BlockSpec((tm, W), lambda i: (i, 0)),
                  pl.BlockSpec((tm, W), lambda i: (i, 1)),
                  pl.BlockSpec((1, W), lambda i: (0, 0)),
                  pl.BlockSpec((1, W), lambda i: (0, 0)),
                  pl.BlockSpec((H, CHUNK, CHUNK), lambda i: (0, 0, 0)),
                  pl.BlockSpec((CHUNK, H), lambda i: (0, 0))],
        out_specs=pl.BlockSpec((tm, W), lambda i: (i, 0)),
        out_shape=jax.ShapeDtypeStruct((M, W), F32 if precise else BF16),
        compiler_params=_params(1),
        name="gmlp_chunk",
    )(z, z, ln_g.reshape(1, W), ln_b.reshape(1, W), ws, bs.T)


def _delta_kernel(zq_ref, zk_ref, zv_ref, zg_ref, gz_ref, cw_ref, alog_ref, dtb_ref, og_ref,
                  o_ref, sfin_ref, xbuf, state, *, n_heads, precise):
    i = pl.program_id(1)
    tm = zq_ref.shape[0]
    H = n_heads
    W = H * LANE
    C = DN_CHUNK
    HC = H * C

    @pl.when(i == 0)
    def _():
        xbuf[0:8, :] = jnp.zeros((8, 3 * W), F32)
        state[...] = jnp.zeros(state.shape, F32)

    xbuf[8:8 + tm, 0:W] = zq_ref[...]
    xbuf[8:8 + tm, W:2 * W] = zk_ref[...]
    xbuf[8:8 + tm, 2 * W:3 * W] = zv_ref[...]
    conv = xbuf[8:8 + tm, :] * cw_ref[3:4, :]
    for s in range(1, CONV_W):
        conv = conv + xbuf[8 - s:8 - s + tm, :] * cw_ref[3 - s:4 - s, :]
    tail = xbuf[tm:tm + 8, :]
    xbuf[0:8, :] = tail
    qkv = _silu(conv)

    zg = zg_ref[...]
    g_all = -jnp.exp(alog_ref[...]) * jax.nn.softplus(zg + dtb_ref[...])
    beta_all = jax.nn.sigmoid(zg)

    r = lax.broadcasted_iota(jnp.int32, (C, C), 0)
    c = lax.broadcasted_iota(jnp.int32, (C, C), 1)
    tril_bf = jnp.where(r >= c, 1.0, 0.0).astype(BF16)
    rr = lax.broadcasted_iota(jnp.int32, (HC, HC), 0)
    cc = lax.broadcasted_iota(jnp.int32, (HC, HC), 1)
    same = (rr // C) == (cc // C)
    incl = jnp.logical_and(same, rr >= cc)
    strict = jnp.logical_and(same, rr > cc)
    eye = rr == cc
    eye_f = jnp.where(eye, 1.0, 0.0)

    for c0 in range(0, tm, C):
        rows = slice(c0, c0 + C)
        gc_all = _dot_sel(tril_bf, g_all[rows])
        qs, ks, vs, gcs, bts, gls = [], [], [], [], [], []
        for h in range(H):
            q = qkv[rows, h * LANE:(h + 1) * LANE]
            k = qkv[rows, W + h * LANE:W + (h + 1) * LANE]
            qs.append(q * lax.rsqrt(jnp.sum(q * q, axis=-1, keepdims=True) + EPS) * (LANE ** -0.5))
            ks.append(k * lax.rsqrt(jnp.sum(k * k, axis=-1, keepdims=True) + EPS))
            vs.append(qkv[rows, 2 * W + h * LANE:2 * W + (h + 1) * LANE])
            gc = gc_all[:, h:h + 1]
            gcs.append(gc)
            gls.append(jnp.broadcast_to(gc[C - 1:C, :], (C, 1)))
            bts.append(beta_all[rows, H + h:H + h + 1])
        Q = jnp.concatenate(qs, axis=0)
        K = jnp.concatenate(ks, axis=0)
        V = jnp.concatenate(vs, axis=0)
        GC = jnp.concatenate(gcs, axis=0)
        GL = jnp.concatenate(gls, axis=0)
        BT = jnp.concatenate(bts, axis=0)
        gc_row = jnp.sum(jnp.where(eye, jnp.broadcast_to(GC, (HC, HC)), 0.0), axis=0, keepdims=True)
        decay = jnp.exp(jnp.where(incl, GC - gc_row, NEG))
        EG = jnp.exp(GC)
        KB = K * BT
        L = jnp.where(strict, _mm(KB, K, precise, _dot_nt) * decay, 0.0)
        T = eye_f - L
        Ph, Pl = _split2(L)
        for _ in range(5):
            P = _dot(Ph, Ph) + (_dot(Ph, Pl) + _dot(Pl, Ph))
            Ph, Pl = _split2(P)
            Th, Tl = _split2(T)
            T = T + (_dot(Th, Ph) + (_dot(Th, Pl) + _dot(Tl, Ph)))
        TV = _mm(T, jnp.concatenate([V * BT, KB * EG], axis=1), precise)
        value = TV[:, 0:LANE]
        kcd = TV[:, LANE:2 * LANE]
        QK = jnp.where(incl, _mm(Q, K, precise, _dot_nt) * decay, 0.0)
        QG = Q * EG
        KDEC = K * jnp.exp(GL - GC)
        vnew, ointer = [], []
        for h in range(H):
            hs = slice(h * C, (h + 1) * C)
            St = state[h]
            both = _mm(jnp.concatenate([kcd[hs], QG[hs]], axis=0), St, precise)
            vn = value[hs] - both[0:C]
            vnew.append(vn)
            ointer.append(both[C:2 * C])
            state[h] = St * jnp.exp(gls[h][0:1, :]) + _mm(KDEC[hs], vn, precise, _dot_tn)
        O = jnp.concatenate(ointer, axis=0) + _mm(QK, jnp.concatenate(vnew, axis=0), precise)
        for h in range(H):
            ls = slice(h * LANE, (h + 1) * LANE)
            o = O[h * C:(h + 1) * C]
            on = o * lax.rsqrt(jnp.mean(o * o, axis=-1, keepdims=True) + EPS) * og_ref[...]
            o_ref[rows, ls] = (on * _silu(gz_ref[rows, ls])).astype(o_ref.dtype)

    @pl.when(i == pl.num_programs(1) - 1)
    def _():
        sfin_ref[0] = state[...]


def _delta_chunked(z, z_tail, cw, a_log, dtb, og, *, batch, seq, tm, precise):
    H = a_log.shape[0]
    W = H * LANE
    nblk = seq // tm
    row = lambda b, i: b * nblk + i
    pad = lambda t: jnp.zeros((1, LANE), F32).at[0, :H].set(t)
    return pl.pallas_call(
        functools.partial(_delta_kernel, n_heads=H, precise=precise),
        grid=(batch, nblk),
        in_specs=[pl.BlockSpec((tm, W), lambda b, i: (row(b, i), 2)),
                  pl.BlockSpec((tm, W), lambda b, i: (row(b, i), 3)),
                  pl.BlockSpec((tm, W), lambda b, i: (row(b, i), 4)),
                  pl.BlockSpec((tm, LANE), lambda b, i: (row(b, i), 0)),
                  pl.BlockSpec((tm, W), lambda b, i: (row(b, i), 5)),
                  pl.BlockSpec((CONV_W, 3 * W), lambda b, i: (0, 0)),
                  pl.BlockSpec((1, LANE), lambda b, i: (0, 0)),
                  pl.BlockSpec((1, LANE), lambda b, i: (0, 0)),
                  pl.BlockSpec((1, LANE), lambda b, i: (0, 0))],
        out_specs=[pl.BlockSpec((tm, W), lambda b, i: (row(b, i), 0)),
                   pl.BlockSpec((1, H, LANE, LANE), lambda b, i: (b, 0, 0, 0))],
        out_shape=[jax.ShapeDtypeStruct((batch * seq, W), F32 if precise else BF16),
                   jax.ShapeDtypeStruct((batch, H, LANE, LANE), F32)],
        scratch_shapes=[pltpu.VMEM((tm + 8, 3 * W), F32), pltpu.VMEM((H, LANE, LANE), F32)],
        compiler_params=_params(2),
        name="delta_chunked",
    )(z, z, z, z_tail, z, cw, pad(a_log), pad(dtb), og.reshape(1, LANE))


def _even_step_kernel(zu_ref, zv_ref, zq_ref, zk_ref, zvv_ref, gz_ref, zg_ref, cb_ref, s0_ref,
                      lng_ref, lnb_ref, w00_ref, b0_ref, cw_ref, alog_ref, dtb_ref, og_ref,
                      ya_ref, vrow_ref, ob_ref, cnew_ref, snew_ref, *, n_heads):
    nb = zu_ref.shape[0]
    W = n_heads * LANE
    u = _gelu(zu_ref[...])
    vg = _gelu(zv_ref[...])
    raw = jnp.concatenate([zq_ref[...], zk_ref[...], zvv_ref[...]], axis=1)
    cb = cb_ref[...]
    conv = raw * cw_ref[3:4, :]
    for s in range(CONV_W - 1):
        conv = conv + cb[:, s * 3 * W:(s + 1) * 3 * W] * cw_ref[s:s + 1, :]
    cnew_ref[:, 0:6 * W] = cb[:, 3 * W:9 * W]
    cnew_ref[:, 6 * W:9 * W] = raw
    qkv = _silu(conv)
    zg = zg_ref[...]
    g_all = -jnp.exp(alog_ref[...]) * jax.nn.softplus(zg + dtb_ref[...])
    beta_all = jax.nn.sigmoid(zg)
    a_all = jnp.exp(g_all)
    r = lax.broadcasted_iota(jnp.int32, (LANE, LANE), 0)
    c = lax.broadcasted_iota(jnp.int32, (LANE, LANE), 1)
    eye = r == c

    def col(row):
        return jnp.sum(jnp.where(eye, jnp.broadcast_to(row, (LANE, LANE)), 0.0), axis=1, keepdims=True)

    for h in range(n_heads):
        ls = slice(h * LANE, (h + 1) * LANE)
        vh = vg[:, ls]
        mu = jnp.mean(vh, axis=-1, keepdims=True)
        d = vh - mu
        var = jnp.mean(d * d, axis=-1, keepdims=True)
        vn = d * lax.rsqrt(var + EPS) * lng_ref[:, ls] + lnb_ref[:, ls]
        vrow_ref[:, ls] = vn
        ya_ref[:, ls] = u[:, ls] * (vn * w00_ref[:, ls] + b0_ref[:, ls])
        q = qkv[:, h * LANE:(h + 1) * LANE]
        k = qkv[:, W + h * LANE:W + (h + 1) * LANE]
        v = qkv[:, 2 * W + h * LANE:2 * W + (h + 1) * LANE]
        q = q * lax.rsqrt(jnp.sum(q * q, axis=-1, keepdims=True) + EPS) * (LANE ** -0.5)
        k = k * lax.rsqrt(jnp.sum(k * k, axis=-1, keepdims=True) + EPS)
        for j in range(nb):
            kc = col(k[j:j + 1])
            qc = col(q[j:j + 1])
            Sd = s0_ref[j, h] * a_all[j:j + 1, h:h + 1]
            kS = jnp.sum(kc * Sd, axis=0, keepdims=True)
            delta = (v[j:j + 1] - kS) * beta_all[j:j + 1, n_heads + h:n_heads + h + 1]
            Sn = Sd + kc * delta
            snew_ref[j, h] = Sn
            o = jnp.sum(qc * Sn, axis=0, keepdims=True)
            on = o * lax.rsqrt(jnp.mean(o * o, axis=-1, keepdims=True) + EPS) * og_ref[...]
            ob_ref[j:j + 1, ls] = on * _silu(gz_ref[j:j + 1, ls])


def _even_step(z, z_tail, conv_buf, s0, ln_g, ln_b, ws, bs, cw, a_log, dtb, og, *, nb):
    Bs = z.shape[0]
    H = a_log.shape[0]
    W = H * LANE
    pad = lambda t: jnp.zeros((1, LANE), F32).at[0, :H].set(t)
    w00 = jnp.repeat(ws[:, 0, 0], LANE).reshape(1, W)
    b0 = jnp.repeat(bs[:, 0], LANE).reshape(1, W)
    cb = conv_buf.reshape(Bs, (CONV_W - 1) * 3 * W)
    zspec = lambda j: pl.BlockSpec((nb, W), lambda i: (i, j))
    vec = lambda n: pl.BlockSpec((1, n), lambda i: (0, 0))
    ya, vrow, ob, cnew, snew = pl.pallas_call(
        functools.partial(_even_step_kernel, n_heads=H),
        grid=(Bs // nb,),
        in_specs=[zspec(0), zspec(1), zspec(2), zspec(3), zspec(4), zspec(5),
                  pl.BlockSpec((nb, LANE), lambda i: (i, 0)),
                  pl.BlockSpec((nb, 9 * W), lambda i: (i, 0)),
                  pl.BlockSpec((nb, H, LANE, LANE), lambda i: (i, 0, 0, 0)),
                  vec(W), vec(W), vec(W), vec(W),
                  pl.BlockSpec((CONV_W, 3 * W), lambda i: (0, 0)),
                  vec(LANE), vec(LANE), vec(LANE)],
        out_specs=[pl.BlockSpec((nb, W), lambda i: (i, 0)),
                   pl.BlockSpec((nb, W), lambda i: (i, 0)),
                   pl.BlockSpec((nb, W), lambda i: (i, 0)),
                   pl.BlockSpec((nb, 9 * W), lambda i: (i, 0)),
                   pl.BlockSpec((nb, H, LANE, LANE), lambda i: (i, 0, 0, 0))],
        out_shape=[jax.ShapeDtypeStruct((Bs, W), F32),
                   jax.ShapeDtypeStruct((Bs, W), F32),
                   jax.ShapeDtypeStruct((Bs, W), F32),
                   jax.ShapeDtypeStruct((Bs, 9 * W), F32),
                   jax.ShapeDtypeStruct((Bs, H, LANE, LANE), F32)],
        compiler_params=_params(1),
        name="even_step_sample",
    )(z, z, z, z, z, z, z_tail, cb, s0, ln_g.reshape(1, W), ln_b.reshape(1, W), w00, b0, cw,
      pad(a_log), pad(dtb), og.reshape(1, LANE))
    return ya, vrow, ob, cnew.reshape(Bs, CONV_W - 1, 3 * W), snew


def _logf_kernel(f_ref, bf_ref, lf_ref, cum_ref, carry):
    @pl.when(pl.program_id(1) == 0)
    def _():
        carry[...] = jnp.zeros(carry.shape, F32)

    tm = f_ref.shape[0]
    lf = jax.nn.log_sigmoid(f_ref[...] + bf_ref[...])
    lf_ref[...] = lf
    r = lax.broadcasted_iota(jnp.int32, (tm, tm), 0)
    c = lax.broadcasted_iota(jnp.int32, (tm, tm), 1)
    tril_bf = jnp.where(r >= c, 1.0, 0.0).astype(BF16)
    cum = _dot_sel(tril_bf, lf) + carry[...]
    cum_ref[...] = cum
    carry[...] = cum[tm - 1:tm, :]


def _logf_cumsum(z_tail, b_f_pad, *, batch, seq, tm):
    nblk = seq // tm
    return pl.pallas_call(
        _logf_kernel,
        grid=(batch, nblk),
        in_specs=[pl.BlockSpec((tm, LANE), lambda b, i: (b * nblk + i, 0)),
                  pl.BlockSpec((1, LANE), lambda b, i: (0, 0))],
        out_specs=[pl.BlockSpec((tm, LANE), lambda b, i: (b * nblk + i, 0)),
                   pl.BlockSpec((tm, LANE), lambda b, i: (b * nblk + i, 0))],
        out_shape=[jax.ShapeDtypeStruct((batch * seq, LANE), F32),
                   jax.ShapeDtypeStruct((batch * seq, LANE), F32)],
        scratch_shapes=[pltpu.VMEM((1, LANE), F32)],
        compiler_params=_params(2),
        name="logf_cumsum",
    )(z_tail, b_f_pad)


def _pair_norm(x, g, lo, dh):
    x2 = x * x
    sa = jnp.sum(jnp.where(lo, x2, 0.0), axis=-1, keepdims=True)
    sb = jnp.sum(jnp.where(lo, 0.0, x2), axis=-1, keepdims=True)
    ms = jnp.where(lo, sa, sb) * (1.0 / dh)
    return x * lax.rsqrt(ms + EPS) * g


def _fox_kernel(q_ref, k_ref, v_ref, f_ref, qg_ref, kg_ref, o_ref, kn_ref, *scr, tq, dh, precise):
    khi, vhi = scr[0], scr[1]
    klo, vlo = (scr[2], scr[3]) if precise else (None, None)
    i = pl.program_id(2)
    S = k_ref.shape[1]
    lo = lax.broadcasted_iota(jnp.int32, (1, LANE), 1) < dh

    def put(hi_ref, lo_ref, rows, val):
        hi = val.astype(BF16)
        hi_ref[rows, :] = hi
        if precise:
            lo_ref[rows, :] = (val - hi.astype(F32)).astype(BF16)

    @pl.when(i == 0)
    def _():
        for c0 in range(0, S, tq):
            rows = slice(c0, c0 + tq)
            kn = _pair_norm(k_ref[0, rows, :], kg_ref[...], lo, dh)
            kn_ref[0, rows, :] = kn
            put(khi, klo, rows, kn)
            put(vhi, vlo, rows, v_ref[0, rows, :])

    q = _pair_norm(q_ref[0], qg_ref[...], lo, dh) * (dh ** -0.5)
    qs = tuple(_split2(qq) if precise else (qq.astype(BF16), None)
               for qq in (jnp.where(lo, q, 0.0), jnp.where(lo, 0.0, q)))
    rr = lax.broadcasted_iota(jnp.int32, (tq, tq), 0)
    cc = lax.broadcasted_iota(jnp.int32, (tq, tq), 1)
    causal = cc <= rr

    def chunk(j, carry, masked):
        rows = pl.ds(pl.multiple_of(j * tq, tq), tq)
        kh, vh = khi[rows, :], vhi[rows, :]
        out = []
        for hh in range(2):
            m, l, acc = carry[hh]
            fj = f_ref[0, 0, hh, pl.ds(j, 1), :]
            qh, ql = qs[hh]
            s = _dot_nt(qh, kh)
            if precise:
                s = s + (_dot_nt(qh, klo[rows, :]) + _dot_nt(ql, kh))
            s = s - fj
            if masked:
                s = jnp.where(causal, s, NEG)
            m_new = jnp.maximum(m, jnp.max(s, axis=-1, keepdims=True))
            alpha = jnp.exp(m - m_new)
            p = jnp.exp(s - m_new)
            l = alpha * l + jnp.sum(p, axis=-1, keepdims=True)
            ph = p.astype(BF16)
            pv = _dot(ph, vh)
            if precise:
                pl_ = (p - ph.astype(F32)).astype(BF16)
                pv = pv + (_dot(ph, vlo[rows, :]) + _dot(pl_, vh))
            acc = alpha * acc + pv
            out.append((m_new, l, acc))
        return tuple(out)

    init = tuple((jnp.full((tq, 1), NEG, F32), jnp.zeros((tq, 1), F32), jnp.zeros((tq, LANE), F32))
                 for _ in range(2))
    carry = lax.fori_loop(0, i, lambda j, cr: chunk(j, cr, False), init)
    (ma, la, acca), (mb, lb, accb) = chunk(i, carry, True)
    o_ref[0] = jnp.where(lo, acca / la, accb / lb).astype(o_ref.dtype)


def _fox_prompt(z3, fcum, qg, kg, *, tq, precise):
    B, S, N3 = z3.shape
    dh = qg.shape[0]
    H = N3 // (3 * dh)
    HP = H // 2
    nq = S // tq
    f5 = fcum.reshape(B, HP, 2, nq, tq)
    g2 = lambda g: jnp.concatenate([g, g]).reshape(1, LANE)
    return pl.pallas_call(
        functools.partial(_fox_kernel, tq=tq, dh=dh, precise=precise),
        grid=(B, HP, nq),
        in_specs=[pl.BlockSpec((1, tq, LANE), lambda b, p, i: (b, i, p)),
                  pl.BlockSpec((1, S, LANE), lambda b, p, i: (b, 0, HP + p)),
                  pl.BlockSpec((1, S, LANE), lambda b, p, i: (b, 0, 2 * HP + p)),
                  pl.BlockSpec((1, 1, 2, nq, tq), lambda b, p, i: (b, p, 0, 0, 0)),
                  pl.BlockSpec((1, LANE), lambda b, p, i: (0, 0)),
                  pl.BlockSpec((1, LANE), lambda b, p, i: (0, 0))],
        out_specs=[pl.BlockSpec((1, tq, LANE), lambda b, p, i: (b, i, p)),
                   pl.BlockSpec((1, S, LANE), lambda b, p, i: (b, 0, p))],
        out_shape=[jax.ShapeDtypeStruct((B, S, H * dh), F32 if precise else BF16),
                   jax.ShapeDtypeStruct((B, S, H * dh), F32)],
        scratch_shapes=[pltpu.VMEM((S, LANE), BF16), pltpu.VMEM((S, LANE), BF16)] * (2 if precise else 1),
        compiler_params=_params(3),
        name="fox_attention_prompt",
    )(z3, z3, z3, f5, g2(qg), g2(kg))


def _head_sel(n_heads, dh):
    W = n_heads * dh
    col_head = jnp.arange(W, dtype=jnp.int32) // dh
    G = (col_head[:, None] == jnp.arange(LANE, dtype=jnp.int32)[None, :]).astype(BF16)
    return G, G.T


def _qknorm_kernel(q_ref, k_ref, f_ref, qg_ref, kg_ref, bf_ref, g_ref, e_ref, qo_ref, ko_ref, lf_ref, *, dh):
    G = g_ref[...]
    E = e_ref[...]

    def norm(x, gain):
        ms = _x_dot_sel(x * x, G) * (1.0 / dh)
        return x * _x_dot_sel(lax.rsqrt(ms + EPS), E) * gain

    qo_ref[...] = norm(q_ref[...], qg_ref[...]) * (dh ** -0.5)
    ko_ref[...] = norm(k_ref[...], kg_ref[...])
    lf_ref[...] = jax.nn.log_sigmoid(f_ref[...] + bf_ref[...])


def _qknorm_sample(z, z_tail, qg, kg, b_f_pad, G, E, *, n_heads):
    Bs = z.shape[0]
    dh = qg.shape[0]
    W = n_heads * dh
    tile = lambda g: jnp.tile(g, n_heads).reshape(1, W)
    full = lambda a: pl.BlockSpec(a.shape, lambda i: (0,) * a.ndim)
    return pl.pallas_call(
        functools.partial(_qknorm_kernel, dh=dh),
        grid=(1,),
        in_specs=[pl.BlockSpec((Bs, W), lambda i: (0, 0)),
                  pl.BlockSpec((Bs, W), lambda i: (0, 1)),
                  full(z_tail), pl.BlockSpec((1, W), lambda i: (0, 0)), pl.BlockSpec((1, W), lambda i: (0, 0)),
                  full(b_f_pad), full(G), full(E)],
        out_specs=[pl.BlockSpec((Bs, W), lambda i: (0, 0)),
                   pl.BlockSpec((Bs, W), lambda i: (0, 0)),
                   pl.BlockSpec((Bs, LANE), lambda i: (0, 0))],
        out_shape=[jax.ShapeDtypeStruct((Bs, W), F32), jax.ShapeDtypeStruct((Bs, W), F32),
                   jax.ShapeDtypeStruct((Bs, LANE), F32)],
        compiler_params=_params(1),
        name="qknorm_sample",
    )(z, z, z_tail, tile(qg), tile(kg), b_f_pad, G, E)


def _decode_kernel(pt_ref, q_ref, kn_ref, vn_ref, lfn_ref, g_ref, *rest, pages, n_heads, dh):
    H = n_heads
    k_refs = rest[:pages]
    v_refs = rest[pages:2 * pages]
    lf_refs = rest[2 * pages:3 * pages]
    o_ref = rest[3 * pages]
    qb, acc, m_scr, l_scr, c_scr, s_scr, p_scr, a_scr = rest[3 * pages + 1:]
    j = pl.program_id(1)
    P = k_refs[0].shape[4]
    lo = lax.broadcasted_iota(jnp.int32, (1, LANE), 1) < dh

    def lane_col(row):
        return jnp.broadcast_to(row, (LANE, LANE)).T

    @pl.when(j == 0)
    def _():
        q = q_ref[0]
        for hp in range(H // 2):
            t = lane_col(q[:, hp * LANE:(hp + 1) * LANE])
            qb[2 * hp] = t[0:dh]
            qb[2 * hp + 1] = t[dh:2 * dh]
        m_scr[...] = jnp.full(m_scr.shape, NEG, F32)
        l_scr[...] = jnp.zeros(l_scr.shape, F32)
        c_scr[...] = jnp.zeros(c_scr.shape, F32)
        acc[...] = jnp.zeros(acc.shape, F32)

    r = lax.broadcasted_iota(jnp.int32, (P, P), 0)
    c = lax.broadcasted_iota(jnp.int32, (P, P), 1)
    triu_bf = jnp.where(r <= c, 1.0, 0.0).astype(BF16)

    for p in range(pages):
        for h in range(H):
            s_scr[h:h + 1, :] = jnp.sum(qb[h] * k_refs[p][0, 0, h], axis=0, keepdims=True)
        fcum = _x_dot_sel(lf_refs[p][0, 0], triu_bf) + c_scr[...]
        c_scr[...] = jnp.broadcast_to(fcum[:, P - 1:P], (H, P))
        s2 = s_scr[...] - fcum
        m_old = m_scr[...]
        m_new = jnp.maximum(m_old, jnp.max(s2, axis=1, keepdims=True))
        alpha = jnp.exp(m_old - m_new)
        pr = jnp.exp(s2 - m_new)
        l_scr[...] = alpha * l_scr[...] + jnp.sum(pr, axis=1, keepdims=True)
        m_scr[...] = m_new
        p_scr[...] = pr
        a_scr[...] = alpha
        for h in range(H):
            acc[h] = acc[h] * a_scr[h:h + 1, :] + p_scr[h:h + 1, :] * v_refs[p][0, 0, h]

    @pl.when(j == pl.num_programs(1) - 1)
    def _():
        q = q_ref[0]
        vn = vn_ref[0]
        prod = jnp.broadcast_to(q * kn_ref[0], (8, q.shape[1]))
        ph, plo = _split2(prod)
        s_row = (_dot(ph, g_ref[...]) + _dot(plo, g_ref[...]))[0:1, :]
        s2 = lane_col(s_row)[0:H] - (c_scr[...] + lane_col(lfn_ref[0])[0:H])
        m_old = m_scr[...]
        m_new = jnp.maximum(m_old, s2)
        alpha = jnp.exp(m_old - m_new)
        pn = jnp.exp(s2 - m_new)
        l = alpha * l_scr[...] + pn
        for hp in range(H // 2):
            both = jnp.concatenate([acc[2 * hp], acc[2 * hp + 1]], axis=0)
            past = jnp.sum(both.T, axis=0, keepdims=True)
            pick = lambda x: jnp.where(lo, x[2 * hp:2 * hp + 1, :], x[2 * hp + 1:2 * hp + 2, :])
            o = (past * pick(alpha) + pick(pn) * vn[:, hp * LANE:(hp + 1) * LANE]) / pick(l)
            o_ref[0, :, hp * LANE:(hp + 1) * LANE] = o


def _decode_attention(page_table, q, kn, vn, lfn, cache_kt, cache_vt, cache_lft, G, *, layer, n_heads):
    Bs, W = q.shape
    npg = page_table.shape[1]
    H, dh, P = cache_kt.shape[2:]
    pp = PAGES_PER_STEP
    pt = page_table.reshape(-1)

    def page(p):
        return lambda b, j, pt_ref: (layer, pt_ref[b * npg + j * pp + p], 0, 0, 0)

    def page4(p):
        return lambda b, j, pt_ref: (layer, pt_ref[b * npg + j * pp + p], 0, 0)

    row = lambda n: pl.BlockSpec((1, 1, n), lambda b, j, pt_ref: (b, 0, 0))
    in_specs = [row(W), row(W), row(W), row(LANE),
                pl.BlockSpec(G.shape, lambda b, j, pt_ref: (0, 0))]
    in_specs += [pl.BlockSpec((1, 1, H, dh, P), page(p)) for p in range(pp)]
    in_specs += [pl.BlockSpec((1, 1, H, dh, P), page(p)) for p in range(pp)]
    in_specs += [pl.BlockSpec((1, 1, H, P), page4(p)) for p in range(pp)]
    r3 = lambda a: a.reshape(Bs, 1, a.shape[1])
    hp_tile = lambda: pltpu.VMEM((H, P), F32)
    out = pl.pallas_call(
        functools.partial(_decode_kernel, pages=pp, n_heads=n_heads, dh=dh),
        grid_spec=pltpu.PrefetchScalarGridSpec(
            num_scalar_prefetch=1,
            grid=(Bs, npg // pp),
            in_specs=in_specs,
            out_specs=pl.BlockSpec((1, 1, W), lambda b, j, pt_ref: (b, 0, 0)),
            scratch_shapes=[pltpu.VMEM((H, dh, P), F32), pltpu.VMEM((H, dh, P), F32),
                            hp_tile(), hp_tile(), hp_tile(), hp_tile(), hp_tile(), hp_tile()]),
        out_shape=jax.ShapeDtypeStruct((Bs, 1, W), F32),
        compiler_params=_params(2),
        name="fox_decode_paged",
    )(pt, r3(q), r3(kn), r3(vn), r3(lfn), G,
      *([cache_kt] * pp), *([cache_vt] * pp), *([cache_lft] * pp))
    return out.reshape(Bs, W)


def _router_kernel(x_ref, g_ref, sc_ref, sh_ref, wr_ref, br_ref, h_ref, lg_ref):
    h = _rms_mod(x_ref[...], g_ref[...], sc_ref[0], sh_ref[0])
    h_ref[...] = h
    lg_ref[...] = _dot3(h, wr_ref[...]) + br_ref[...]


def _router(x, g, sc, sh, wr_pad, br_pad, *, tm, rows_per_mod):
    M, D = x.shape
    bpm = rows_per_mod // tm
    R = sc.shape[1]
    mod = pl.BlockSpec((1, R, D), lambda i: (i // bpm, 0, 0))
    return pl.pallas_call(
        _router_kernel,
        grid=(M // tm,),
        in_specs=[pl.BlockSpec((tm, D), lambda i: (i, 0)),
                  pl.BlockSpec((1, D), lambda i: (0, 0)),
                  mod, mod,
                  pl.BlockSpec((D, LANE), lambda i: (0, 0)),
                  pl.BlockSpec((1, LANE), lambda i: (0, 0))],
        out_specs=[pl.BlockSpec((tm, D), lambda i: (i, 0)),
                   pl.BlockSpec((tm, LANE), lambda i: (i, 0))],
        out_shape=[jax.ShapeDtypeStruct((M, D), F32), jax.ShapeDtypeStruct((M, LANE), F32)],
        compiler_params=_params(1),
        name="router",
    )(x, g, sc, sh, wr_pad, br_pad)


def _moe_kernel(be_ref, na_ref, x_ref, w1_ref, b1_ref, w2_ref, b2_ref, o_ref, *scr, f_chunk, precise):
    w1hi, w2hi = scr[0], scr[1]
    w1lo, w2lo = (scr[2], scr[3]) if precise else (None, None)
    i = pl.program_id(0)
    active = i < na_ref[0]
    e = be_ref[i]
    prev = be_ref[jnp.maximum(i - 1, 0)]
    first = jnp.logical_or(i == 0, e != prev)
    FF = w2hi.shape[0]

    @pl.when(jnp.logical_and(active, first))
    def _():
        _cast_rows(lambda sl: w1_ref[0, 0, sl, :], w1hi, w1lo)
        _cast_rows(lambda sl: w2_ref[0, 0, sl, :], w2hi, w2lo)

    @pl.when(active)
    def _():
        x = x_ref[...]
        xh = x.astype(BF16)
        xl = (x - xh.astype(F32)).astype(BF16) if precise else None

        def mm1(cols):
            out = _dot(xh, w1hi[:, cols])
            if precise:
                out = out + (_dot(xh, w1lo[:, cols]) + _dot(xl, w1hi[:, cols]))
            return out

        acc = None
        for f0 in range(0, FF, f_chunk):
            glu = mm1(slice(f0, f0 + f_chunk)) + b1_ref[0, 0, :, f0:f0 + f_chunk]
            lin = mm1(slice(FF + f0, FF + f0 + f_chunk)) + b1_ref[0, 0, :, FF + f0:FF + f0 + f_chunk]
            glu = jnp.minimum(glu, SWIGLU_LIMIT)
            lin = jnp.clip(lin, -SWIGLU_LIMIT, SWIGLU_LIMIT)
            a = glu * jax.nn.sigmoid(SWIGLU_ALPHA * glu) * (lin + 1.0)
            t = _mm_w(a, w2hi[f0:f0 + f_chunk, :], w2lo[f0:f0 + f_chunk, :] if precise else None)
            acc = t if acc is None else acc + t
        o_ref[...] = acc + b2_ref[0, 0]

    @pl.when(jnp.logical_not(active))
    def _():
        o_ref[...] = jnp.zeros(o_ref.shape, F32)


def _moe_experts(blk_e, n_active, xb, w1, b1, w2, b2, *, layer, bm, precise):
    NS = xb.shape[0]
    L, E, D, F2 = w1.shape
    FF = F2 // 2
    nb = NS // bm
    return pl.pallas_call(
        functools.partial(_moe_kernel, f_chunk=512, precise=precise),
        grid_spec=pltpu.PrefetchScalarGridSpec(
            num_scalar_prefetch=2,
            grid=(nb,),
            in_specs=[pl.BlockSpec((bm, D), lambda i, be, na: (i, 0)),
                      pl.BlockSpec((1, 1, D, F2), lambda i, be, na: (layer, be[i], 0, 0)),
                      pl.BlockSpec((1, 1, 1, F2), lambda i, be, na: (layer, be[i], 0, 0)),
                      pl.BlockSpec((1, 1, FF, D), lambda i, be, na: (layer, be[i], 0, 0)),
                      pl.BlockSpec((1, 1, 1, D), lambda i, be, na: (layer, be[i], 0, 0))],
            out_specs=pl.BlockSpec((bm, D), lambda i, be, na: (i, 0)),
            scratch_shapes=[pltpu.VMEM((D, F2), BF16), pltpu.VMEM((FF, D), BF16)] * (2 if precise else 1)),
        out_shape=jax.ShapeDtypeStruct((NS, D), F32),
        compiler_params=_params(1),
        name="moe_experts",
    )(blk_e, n_active, xb, w1, b1.reshape(L, E, 1, F2), w2, b2.reshape(L, E, 1, D))


def _moe_ffn(h_all, logits, w1, b1, w2, b2, *, layer, precise):
    T = h_all.shape[0]
    D = w2.shape[3]
    bm = MOE_TILE
    top_v, top_i = lax.top_k(logits, TOP_K)
    gate = jax.nn.softmax(top_v, axis=-1)
    n = T * TOP_K
    e_km = top_i.T.reshape(-1).astype(jnp.int32)
    t_km = jnp.tile(jnp.arange(T, dtype=jnp.int32), TOP_K)
    experts = jnp.arange(N_EXPERTS, dtype=jnp.int32)
    onehot = (e_km[:, None] == experts[None, :]).astype(jnp.int32)
    csum = jnp.cumsum(onehot, axis=0)
    counts = csum[-1]
    nblk_e = (counts + bm - 1) // bm
    blk_end = jnp.cumsum(nblk_e)
    blk_start = blk_end - nblk_e
    dest = jnp.sum(onehot * (csum - 1 + (blk_start * bm)[None, :]), axis=1)
    nb = -(-(n + N_EXPERTS * (bm - 1)) // bm)
    n_active = blk_end[-1]
    bidx = jnp.minimum(jnp.arange(nb, dtype=jnp.int32), n_active - 1)
    blk_e = jnp.minimum(jnp.sum((blk_end[None, :] <= bidx[:, None]).astype(jnp.int32), axis=1), N_EXPERTS - 1)
    slot_tok = jnp.zeros((nb * bm,), jnp.int32).at[dest].set(t_km, mode="promise_in_bounds", unique_indices=True)
    xb = h_all.at[slot_tok].get(mode="promise_in_bounds")
    yb = _moe_experts(blk_e, n_active.reshape(1).astype(jnp.int32), xb, w1, b1, w2, b2, layer=layer, bm=bm,
                      precise=precise)
    ya = yb.at[dest].get(mode="promise_in_bounds").reshape(TOP_K, T, D)
    return jnp.sum(ya * gate.T[:, :, None], axis=0)


def kernel(x_prompt, x_sample, cache_k, cache_v, cache_logf, state_delta, state_conv, page_table, c_prompt, c_sample, norm_g, w_ada, b_ada, w_in_even, a_ln_g, a_ln_b, a_ws, a_bs, conv_w, A_log, dt_bias, onorm_g, w_out_even, w_in_odd, b_f, q_norm_g, k_norm_g, w_out_odd, w_router, b_router, w_mlp1, b_mlp1, w_mlp2, b_mlp2):
    Bp, Sp, D = x_prompt.shape
    Bs = x_sample.shape[0]
    depth = norm_g.shape[0]
    Mp = Bp * Sp
    a_heads = a_ws.shape[1]
    a_width = a_heads * LANE
    b_heads = A_log.shape[1]
    b_width = b_heads * LANE
    c_heads = b_f.shape[1]
    c_dh = q_norm_g.shape[1]
    c_width = c_heads * c_dh
    tm = ROW_TILE

    xp = x_prompt.reshape(Mp, D)
    xs = x_sample.reshape(Bs, D)
    mods = _ada_all(jnp.concatenate([c_prompt, c_sample], axis=0), w_ada, b_ada)
    ckt = jnp.transpose(cache_k, (0, 1, 3, 4, 2))
    cvt = jnp.transpose(cache_v, (0, 1, 3, 4, 2))
    clft = jnp.transpose(cache_logf, (0, 1, 3, 2))
    G, E = _head_sel(c_heads, c_dh)

    def pad_cols(w):
        return jnp.pad(w, ((0, 0), (0, LANE - w.shape[1])))

    kp_l, vp_l, fp_l, ks_l, vs_l, fs_l = [], [], [], [], [], []
    dp_l, cp_l, ds_l, cs_l, av_l = [], [], [], [], []

    for l in range(depth):
        i = l // 2
        pm = l < PRECISE_MIXER_LAYERS
        m6 = mods[l].reshape(Bp + Bs, 6, D)
        mp = [m6[:Bp, j].reshape(Bp, 1, D) for j in range(6)]
        ms = [m6[Bp:, j].reshape(1, Bs, D) for j in range(6)]
        g1 = norm_g[l, 0].reshape(1, D)
        g2 = norm_g[l, 1].reshape(1, D)
        if l % 2 == 0:
            n_main = 2 * a_width + 4 * b_width
            w_main = w_in_even[i][:, :n_main]
            w_tail = pad_cols(w_in_even[i][:, n_main:])
            zp, ztp = _proj(xp, g1, mp[1], mp[0], w_main, w_tail, tm=tm, rows_per_mod=Sp, precise=pm)
            zs, zts = _proj(xs, g1, ms[1], ms[0], w_main, w_tail, tm=Bs, rows_per_mod=Bs, precise=True)
            ya_p = _gmlp(zp, a_ln_g[i], a_ln_b[i], a_ws[i], a_bs[i], tm=2 * CHUNK, precise=pm)
            ob_p, sfin_p = _delta_chunked(zp, ztp, conv_w[i], A_log[i], dt_bias[i], onorm_g[i],
                                          batch=Bp, seq=Sp, tm=4 * DN_CHUNK, precise=pm)
            ya_s, vrow_s, ob_s, cnew_s, snew_s = _even_step(
                zs, zts, state_conv[i], state_delta[i], a_ln_g[i], a_ln_b[i], a_ws[i], a_bs[i],
                conv_w[i], A_log[i], dt_bias[i], onorm_g[i], nb=8)
            xp = _outproj([ya_p, ob_p], w_out_even[i], xp, mp[2], tm=tm, rows_per_mod=Sp, precise=pm)
            xs = _outproj([ya_s, ob_s], w_out_even[i], xs, ms[2], tm=Bs, rows_per_mod=Bs, precise=True)
            qkv_raw = zp[:, 2 * a_width:2 * a_width + 3 * b_width].reshape(Bp, Sp, 3 * b_width)
            dp_l.append(sfin_p)
            cp_l.append(qkv_raw[:, Sp - (CONV_W - 1):])
            ds_l.append(snew_s)
            cs_l.append(cnew_s)
            av_l.append(vrow_s.reshape(Bs, 1, a_width))
        else:
            n_main = 3 * c_width
            w_main = w_in_odd[i][:, :n_main]
            w_tail = pad_cols(w_in_odd[i][:, n_main:])
            bf_pad = jnp.zeros((1, LANE), F32).at[0, :c_heads].set(b_f[i])
            zp, ztp = _proj(xp, g1, mp[1], mp[0], w_main, w_tail, tm=tm, rows_per_mod=Sp, precise=pm)
            zs, zts = _proj(xs, g1, ms[1], ms[0], w_main, w_tail, tm=Bs, rows_per_mod=Bs, precise=True)
            lf_p, fc_p = _logf_cumsum(ztp, bf_pad, batch=Bp, seq=Sp, tm=512)
            lf_p = lf_p[:, :c_heads].reshape(Bp, Sp, c_heads)
            fcum = jnp.swapaxes(fc_p[:, :c_heads].reshape(Bp, Sp, c_heads), 1, 2)
            o_p, kn_p = _fox_prompt(zp.reshape(Bp, Sp, n_main), fcum, q_norm_g[i], k_norm_g[i], tq=512, precise=pm)
            xp = _outproj([o_p.reshape(Mp, c_width)], w_out_odd[i], xp, mp[2], tm=tm, rows_per_mod=Sp, precise=pm)
            q_s, k_s, lf_s = _qknorm_sample(zs, zts, q_norm_g[i], k_norm_g[i], bf_pad, G, E, n_heads=c_heads)
            v_s = zs[:, 2 * c_width:]
            o_s = _decode_attention(page_table, q_s, k_s, v_s, lf_s, ckt, cvt, clft, G,
                                    layer=i, n_heads=c_heads)
            xs = _outproj([o_s], w_out_odd[i], xs, ms[2], tm=Bs, rows_per_mod=Bs, precise=True)
            kp_l.append(kn_p.reshape(Bp, Sp, c_heads, c_dh))
            vp_l.append(zp[:, 2 * c_width:].reshape(Bp, Sp, c_heads, c_dh))
            fp_l.append(lf_p)
            ks_l.append(k_s.reshape(Bs, 1, c_heads, c_dh))
            vs_l.append(v_s.reshape(Bs, 1, c_heads, c_dh))
            fs_l.append(lf_s[:, :c_heads].reshape(Bs, 1, c_heads))
        wr_pad = pad_cols(w_router[l])
        br_pad = jnp.zeros((1, LANE), F32).at[0, :N_EXPERTS].set(b_router[l])
        hp, lgp = _router(xp, g2, mp[4], mp[3], wr_pad, br_pad, tm=tm, rows_per_mod=Sp)
        hs, lgs = _router(xs, g2, ms[4], ms[3], wr_pad, br_pad, tm=Bs, rows_per_mod=Bs)
        h_all = jnp.concatenate([hp, hs], axis=0)
        lg_all = jnp.concatenate([lgp, lgs], axis=0)[:, :N_EXPERTS]
        y_all = _moe_ffn(h_all, lg_all, w_mlp1, b_mlp1, w_mlp2, b_mlp2, layer=l, precise=l < PRECISE_MOE_LAYERS)
        xp = xp + (mp[5] * y_all[:Mp].reshape(Bp, Sp, D)).reshape(Mp, D)
        xs = xs + ms[5][0] * y_all[Mp:]

    return (xp.reshape(Bp, Sp, D), xs.reshape(Bs, 1, D),
            jnp.stack(kp_l), jnp.stack(vp_l), jnp.stack(fp_l),
            jnp.stack(ks_l), jnp.stack(vs_l), jnp.stack(fs_l),
            jnp.stack(dp_l), jnp.stack(cp_l), jnp.stack(ds_l), jnp.stack(cs_l), jnp.stack(av_l))
```

```python
import functools

import jax
import jax.numpy as jnp
from jax import lax
from jax.experimental import pallas as pl
from jax.experimental.pallas import tpu as pltpu

F32 = jnp.float32
BF16 = jnp.bfloat16

EPS = 1e-6
CHUNK = 128
DN_CHUNK = 64
N_EXPERTS = 32
TOP_K = 4
SWIGLU_ALPHA = 1.702
SWIGLU_LIMIT = 7.0
CONV_W = 4
LANE = 128
NEG = -1e30

VMEM_LIMIT = 56 * 1024 * 1024
ROW_TILE = 512
MOE_TILE = 512
PAGES_PER_STEP = 8
PRECISE_MIXER_LAYERS = 2
PRECISE_MOE_LAYERS = 1


def _params(n_axes):
    return pltpu.CompilerParams(dimension_semantics=("arbitrary",) * n_axes,
                                vmem_limit_bytes=VMEM_LIMIT)


def _dot(a, b):
    return jnp.dot(a, b, preferred_element_type=F32)


def _dot_nt(a, b):
    return lax.dot_general(a, b, (((1,), (1,)), ((), ())), preferred_element_type=F32)


def _dot_tn(a, b):
    return lax.dot_general(a, b, (((0,), (0,)), ((), ())), preferred_element_type=F32)


def _split2(a):
    hi = a.astype(BF16)
    lo = (a - hi.astype(F32)).astype(BF16)
    return hi, lo


def _split3(a):
    hi = a.astype(BF16)
    r = a - hi.astype(F32)
    mid = r.astype(BF16)
    lo = (r - mid.astype(F32)).astype(BF16)
    return hi, mid, lo


def _dot3(a, b):
    ah, al = _split2(a)
    bh, bl = _split2(b)
    return _dot(ah, bh) + (_dot(ah, bl) + _dot(al, bh))


def _mm(a, b, precise, dot=_dot):
    if precise:
        ah, al = _split2(a)
        bh, bl = _split2(b)
        return dot(ah, bh) + (dot(ah, bl) + dot(al, bh))
    return dot(a.astype(BF16), b.astype(BF16))


def _mm_w(a, w_hi, w_lo):
    ah = a.astype(BF16)
    out = _dot(ah, w_hi)
    if w_lo is not None:
        al = (a - ah.astype(F32)).astype(BF16)
        out = out + (_dot(ah, w_lo) + _dot(al, w_hi))
    return out


def _dot_sel(sel_bf16, x):
    hi, mid, lo = _split3(x)
    return _dot(sel_bf16, hi) + (_dot(sel_bf16, mid) + _dot(sel_bf16, lo))


def _x_dot_sel(x, sel_bf16):
    hi, mid, lo = _split3(x)
    return _dot(hi, sel_bf16) + (_dot(mid, sel_bf16) + _dot(lo, sel_bf16))


def _rms_mod(x, g, sc, sh):
    ms = jnp.mean(x * x, axis=-1, keepdims=True)
    return (x * lax.rsqrt(ms + EPS) * g) * (1.0 + sc) + sh


def _gelu(x):
    return 0.5 * x * (1.0 + lax.erf(x * 0.7071067811865476))


def _silu(x):
    return x * jax.nn.sigmoid(x)


def _cast_rows(load, hi_ref, lo_ref, rows=128):
    n = hi_ref.shape[0] // rows

    def body(r, c):
        sl = pl.ds(pl.multiple_of(r * rows, rows), rows)
        w = load(sl)
        hi = w.astype(BF16)
        hi_ref[sl, :] = hi
        if lo_ref is not None:
            lo_ref[sl, :] = (w - hi.astype(F32)).astype(BF16)
        return c

    lax.fori_loop(0, n, body, 0)


def _ada_kernel(c_ref, w_ref, b_ref, o_ref):
    c = c_ref[...]
    o_ref[0] = _dot3(_silu(c), w_ref[0]) + b_ref[0]


def _ada_all(c_all, w_ada, b_ada):
    L, D, N = w_ada.shape
    nb = c_all.shape[0]
    tn = 1536
    return pl.pallas_call(
        _ada_kernel,
        grid=(L, N // tn),
        in_specs=[pl.BlockSpec((nb, D), lambda l, j: (0, 0)),
                  pl.BlockSpec((1, D, tn), lambda l, j: (l, 0, j)),
                  pl.BlockSpec((1, 1, tn), lambda l, j: (l, 0, j))],
        out_specs=pl.BlockSpec((1, nb, tn), lambda l, j: (l, 0, j)),
        out_shape=jax.ShapeDtypeStruct((L, nb, N), F32),
        compiler_params=_params(2),
        name="ada_mod",
    )(c_all, w_ada, b_ada.reshape(L, 1, N))


def _proj_kernel(x_ref, g_ref, sc_ref, sh_ref, w_ref, wt_ref, o_ref, ot_ref, *scr, n_chunk, precise):
    whi, wthi = scr[0], scr[1]
    wlo, wtlo = (scr[2], scr[3]) if precise else (None, None)

    @pl.when(pl.program_id(0) == 0)
    def _():
        _cast_rows(lambda sl: w_ref[sl, :], whi, wlo)
        _cast_rows(lambda sl: wt_ref[sl, :], wthi, wtlo)

    h = _rms_mod(x_ref[...], g_ref[...], sc_ref[0], sh_ref[0])
    hh = h.astype(BF16)
    hl = (h - hh.astype(F32)).astype(BF16) if precise else None

    def mm(w_hi, w_lo, cols):
        out = _dot(hh, w_hi[:, cols])
        if precise:
            out = out + (_dot(hh, w_lo[:, cols]) + _dot(hl, w_hi[:, cols]))
        return out

    for n0 in range(0, o_ref.shape[1], n_chunk):
        o_ref[:, n0:n0 + n_chunk] = mm(whi, wlo, slice(n0, n0 + n_chunk))
    ot_ref[...] = mm(wthi, wtlo, slice(0, ot_ref.shape[1]))


def _proj(x, g, sc, sh, w, w_tail, *, tm, rows_per_mod, precise):
    M, D = x.shape
    N = w.shape[1]
    NT = w_tail.shape[1]
    bpm = rows_per_mod // tm
    R = sc.shape[1]
    mod = pl.BlockSpec((1, R, D), lambda i: (i // bpm, 0, 0))
    parts = 2 if precise else 1
    return pl.pallas_call(
        functools.partial(_proj_kernel, n_chunk=512, precise=precise),
        grid=(M // tm,),
        in_specs=[pl.BlockSpec((tm, D), lambda i: (i, 0)),
                  pl.BlockSpec((1, D), lambda i: (0, 0)),
                  mod, mod,
                  pl.BlockSpec((D, N), lambda i: (0, 0), pipeline_mode=pl.Buffered(1)),
                  pl.BlockSpec((D, NT), lambda i: (0, 0), pipeline_mode=pl.Buffered(1))],
        out_specs=[pl.BlockSpec((tm, N), lambda i: (i, 0)),
                   pl.BlockSpec((tm, NT), lambda i: (i, 0))],
        out_shape=[jax.ShapeDtypeStruct((M, N), F32), jax.ShapeDtypeStruct((M, NT), F32)],
        scratch_shapes=[pltpu.VMEM((D, N), BF16), pltpu.VMEM((D, NT), BF16)] * parts,
        compiler_params=_params(1),
        name="norm_mod_proj",
    )(x, g, sc, sh, w, w_tail)


def _outproj_kernel(*refs, n_in, precise):
    ys = refs[:n_in]
    w_ref, x_ref, gate_ref, o_ref, whi = refs[n_in:n_in + 5]
    wlo = refs[n_in + 5] if precise else None

    @pl.when(pl.program_id(0) == 0)
    def _():
        _cast_rows(lambda sl: w_ref[sl, :], whi, wlo)

    acc = None
    k0 = 0
    for y_ref in ys:
        kk = y_ref.shape[1]
        t = _mm_w(y_ref[...], whi[k0:k0 + kk, :], wlo[k0:k0 + kk, :] if precise else None)
        acc = t if acc is None else acc + t
        k0 += kk
    o_ref[...] = x_ref[...] + gate_ref[0] * acc


def _outproj(ys, w, x, gate, *, tm, rows_per_mod, precise):
    M, D = x.shape
    K = w.shape[0]
    bpm = rows_per_mod // tm
    R = gate.shape[1]
    in_specs = [pl.BlockSpec((tm, y.shape[1]), lambda i: (i, 0)) for y in ys]
    in_specs += [pl.BlockSpec((K, D), lambda i: (0, 0), pipeline_mode=pl.Buffered(1)),
                 pl.BlockSpec((tm, D), lambda i: (i, 0)),
                 pl.BlockSpec((1, R, D), lambda i: (i // bpm, 0, 0))]
    return pl.pallas_call(
        functools.partial(_outproj_kernel, n_in=len(ys), precise=precise),
        grid=(M // tm,),
        in_specs=in_specs,
        out_specs=pl.BlockSpec((tm, D), lambda i: (i, 0)),
        out_shape=jax.ShapeDtypeStruct((M, D), F32),
        scratch_shapes=[pltpu.VMEM((K, D), BF16)] * (2 if precise else 1),
        compiler_params=_params(1),
        name="out_proj_residual",
    )(*ys, w, x, gate)


def _gmlp_kernel(u_ref, v_ref, lng_ref, lnb_ref, ws_ref, bst_ref, o_ref, *, n_heads, precise):
    tm = u_ref.shape[0]
    u = _gelu(u_ref[...])
    v = _gelu(v_ref[...])
    r = lax.broadcasted_iota(jnp.int32, (CHUNK, CHUNK), 0)
    c = lax.broadcasted_iota(jnp.int32, (CHUNK, CHUNK), 1)
    tril = r >= c
    for h in range(n_heads):
        ls = slice(h * LANE, (h + 1) * LANE)
        vh = v[:, ls]
        mu = jnp.mean(vh, axis=-1, keepdims=True)
        d = vh - mu
        var = jnp.mean(d * d, axis=-1, keepdims=True)
        vn = d * lax.rsqrt(var + EPS) * lng_ref[:, ls] + lnb_ref[:, ls]
        wm = jnp.where(tril, ws_ref[h], 0.0)
        bias = bst_ref[:, h:h + 1]
        for c0 in range(0, tm, CHUNK):
            mixed = _mm(wm, vn[c0:c0 + CHUNK], precise) + bias
            o_ref[c0:c0 + CHUNK, ls] = (u[c0:c0 + CHUNK, ls] * mixed).astype(o_ref.dtype)


def _gmlp(z, ln_g, ln_b, ws, bs, *, tm, precise):
    M = z.shape[0]
    H = ws.shape[0]
    W = H * LANE
    return pl.pallas_call(
        functools.partial(_gmlp_kernel, n_heads=H, precise=precise),
        grid=(M // tm,),
        in_specs=[pl.BlockSpec((tm, W), lambda i: (i, 0)),
                  pl.BlockSpec((tm, W), lambda i: (i, 1)),
                  pl.BlockSpec((1, W), lambda i: (0, 0)),
                  pl.BlockSpec((1, W), lambda i: (0, 0)),
                  pl.BlockSpec((H, CHUNK, CHUNK), lambda i: (0, 0, 0)),
                  pl.BlockSpec((CHUNK, H), lambda i: (0, 0))],
        out_specs=pl.BlockSpec((tm, W), lambda i: (i, 0)),
        out_shape=jax.ShapeDtypeStruct((M, W), F32 if precise else BF16),
        compiler_params=_params(1),
        name="gmlp_chunk",
    )(z, z, ln_g.reshape(1, W), ln_b.reshape(1, W), ws, bs.T)


def _delta_kernel(zq_ref, zk_ref, zv_ref, zg_ref, gz_ref, cw_ref, alog_ref, dtb_ref, og_ref,
                  o_ref, sfin_ref, xbuf, state, *, n_heads, precise):
    i = pl.program_id(1)
    tm = zq_ref.shape[0]
    H = n_heads
    W = H * LANE
    C = DN_CHUNK
    HC = H * C

    @pl.when(i == 0)
    def _():
        xbuf[0:8, :] = jnp.zeros((8, 3 * W), F32)
        state[...] = jnp.zeros(state.shape, F32)

    xbuf[8:8 + tm, 0:W] = zq_ref[...]
    xbuf[8:8 + tm, W:2 * W] = zk_ref[...]
    xbuf[8:8 + tm, 2 * W:3 * W] = zv_ref[...]
    conv = xbuf[8:8 + tm, :] * cw_ref[3:4, :]
    for s in range(1, CONV_W):
        conv = conv + xbuf[8 - s:8 - s + tm, :] * cw_ref[3 - s:4 - s, :]
    tail = xbuf[tm:tm + 8, :]
    xbuf[0:8, :] = tail
    qkv = _silu(conv)

    zg = zg_ref[...]
    g_all = -jnp.exp(alog_ref[...]) * jax.nn.softplus(zg + dtb_ref[...])
    beta_all = jax.nn.sigmoid(zg)

    r = lax.broadcasted_iota(jnp.int32, (C, C), 0)
    c = lax.broadcasted_iota(jnp.int32, (C, C), 1)
    tril_bf = jnp.where(r >= c, 1.0, 0.0).astype(BF16)
    rr = lax.broadcasted_iota(jnp.int32, (HC, HC), 0)
    cc = lax.broadcasted_iota(jnp.int32, (HC, HC), 1)
    same = (rr // C) == (cc // C)
    incl = jnp.logical_and(same, rr >= cc)
    strict = jnp.logical_and(same, rr > cc)
    eye = rr == cc
    eye_f = jnp.where(eye, 1.0, 0.0)

    chunks = []
    for c0 in range(0, tm, C):
        rows = slice(c0, c0 + C)
        gc_all = _dot_sel(tril_bf, g_all[rows])
        qs, ks, vs, gcs, bts, gls = [], [], [], [], [], []
        for h in range(H):
            q = qkv[rows, h * LANE:(h + 1) * LANE]
            k = qkv[rows, W + h * LANE:W + (h + 1) * LANE]
            qs.append(q * lax.rsqrt(jnp.sum(q * q, axis=-1, keepdims=True) + EPS) * (LANE ** -0.5))
            ks.append(k * lax.rsqrt(jnp.sum(k * k, axis=-1, keepdims=True) + EPS))
            vs.append(qkv[rows, 2 * W + h * LANE:2 * W + (h + 1) * LANE])
            gc = gc_all[:, h:h + 1]
            gcs.append(gc)
            gls.append(jnp.broadcast_to(gc[C - 1:C, :], (C, 1)))
            bts.append(beta_all[rows, H + h:H + h + 1])
        Q = jnp.concatenate(qs, axis=0)
        K = jnp.concatenate(ks, axis=0)
        V = jnp.concatenate(vs, axis=0)
        GC = jnp.concatenate(gcs, axis=0)
        GL = jnp.concatenate(gls, axis=0)
        BT = jnp.concatenate(bts, axis=0)
        gc_row = jnp.sum(jnp.where(eye, jnp.broadcast_to(GC, (HC, HC)), 0.0), axis=0, keepdims=True)
        decay = jnp.exp(jnp.where(incl, GC - gc_row, NEG))
        EG = jnp.exp(GC)
        KB = K * BT
        L = jnp.where(strict, _mm(KB, K, precise, _dot_nt) * decay, 0.0)
        chunks.append(dict(rows=rows, Q=Q, K=K, GC=GC, GL=GL, gls=gls, decay=decay, EG=EG, KB=KB,
                           VB=V * BT, T=eye_f - L, P=_split2(L)))

    for _ in range(5):
        for ch in chunks:
            Ph, Pl = ch["P"]
            P = _dot(Ph, Ph) + (_dot(Ph, Pl) + _dot(Pl, Ph))
            Ph, Pl = _split2(P)
            Th, Tl = _split2(ch["T"])
            ch["T"] = ch["T"] + (_dot(Th, Ph) + (_dot(Th, Pl) + _dot(Tl, Ph)))
            ch["P"] = (Ph, Pl)

    for ch in chunks:
        rows, Q, K, GC, GL, gls = ch["rows"], ch["Q"], ch["K"], ch["GC"], ch["GL"], ch["gls"]
        decay, EG, KB = ch["decay"], ch["EG"], ch["KB"]
        TV = _mm(ch["T"], jnp.concatenate([ch["VB"], KB * EG], axis=1), precise)
        value = TV[:, 0:LANE]
        kcd = TV[:, LANE:2 * LANE]
        QK = jnp.where(incl, _mm(Q, K, precise, _dot_nt) * decay, 0.0)
        QG = Q * EG
        KDEC = K * jnp.exp(GL - GC)
        vnew, ointer = [], []
        for h in range(H):
            hs = slice(h * C, (h + 1) * C)
            St = state[h]
            both = _mm(jnp.concatenate([kcd[hs], QG[hs]], axis=0), St, precise)
            vn = value[hs] - both[0:C]
            vnew.append(vn)
            ointer.append(both[C:2 * C])
            state[h] = St * jnp.exp(gls[h][0:1, :]) + _mm(KDEC[hs], vn, precise, _dot_tn)
        O = jnp.concatenate(ointer, axis=0) + _mm(QK, jnp.concatenate(vnew, axis=0), precise)
        for h in range(H):
            ls = slice(h * LANE, (h + 1) * LANE)
            o = O[h * C:(h + 1) * C]
            on = o * lax.rsqrt(jnp.mean(o * o, axis=-1, keepdims=True) + EPS) * og_ref[...]
            o_ref[rows, ls] = (on * _silu(gz_ref[rows, ls])).astype(o_ref.dtype)

    @pl.when(i == pl.num_programs(1) - 1)
    def _():
        sfin_ref[0] = state[...]


def _delta_chunked(z, z_tail, cw, a_log, dtb, og, *, batch, seq, tm, precise):
    H = a_log.shape[0]
    W = H * LANE
    nblk = seq // tm
    row = lambda b, i: b * nblk + i
    pad = lambda t: jnp.zeros((1, LANE), F32).at[0, :H].set(t)
    return pl.pallas_call(
        functools.partial(_delta_kernel, n_heads=H, precise=precise),
        grid=(batch, nblk),
        in_specs=[pl.BlockSpec((tm, W), lambda b, i: (row(b, i), 2)),
                  pl.BlockSpec((tm, W), lambda b, i: (row(b, i), 3)),
                  pl.BlockSpec((tm, W), lambda b, i: (row(b, i), 4)),
                  pl.BlockSpec((tm, LANE), lambda b, i: (row(b, i), 0)),
                  pl.BlockSpec((tm, W), lambda b, i: (row(b, i), 5)),
                  pl.BlockSpec((CONV_W, 3 * W), lambda b, i: (0, 0)),
                  pl.BlockSpec((1, LANE), lambda b, i: (0, 0)),
                  pl.BlockSpec((1, LANE), lambda b, i: (0, 0)),
                  pl.BlockSpec((1, LANE), lambda b, i: (0, 0))],
        out_specs=[pl.BlockSpec((tm, W), lambda b, i: (row(b, i), 0)),
                   pl.BlockSpec((1, H, LANE, LANE), lambda b, i: (b, 0, 0, 0))],
        out_shape=[jax.ShapeDtypeStruct((batch * seq, W), F32 if precise else BF16),
                   jax.ShapeDtypeStruct((batch, H, LANE, LANE), F32)],
        scratch_shapes=[pltpu.VMEM((tm + 8, 3 * W), F32), pltpu.VMEM((H, LANE, LANE), F32)],
        compiler_params=_params(2),
        name="delta_chunked",
    )(z, z, z, z_tail, z, cw, pad(a_log), pad(dtb), og.reshape(1, LANE))


def _even_step_kernel(zu_ref, zv_ref, zq_ref, zk_ref, zvv_ref, gz_ref, zg_ref, cb_ref, s0_ref,
                      lng_ref, lnb_ref, w00_ref, b0_ref, cw_ref, alog_ref, dtb_ref, og_ref,
                      ya_ref, vrow_ref, ob_ref, cnew_ref, snew_ref, *, n_heads):
    nb = zu_ref.shape[0]
    W = n_heads * LANE
    u = _gelu(zu_ref[...])
    vg = _gelu(zv_ref[...])
    raw = jnp.concatenate([zq_ref[...], zk_ref[...], zvv_ref[...]], axis=1)
    cb = cb_ref[...]
    conv = raw * cw_ref[3:4, :]
    for s in range(CONV_W - 1):
        conv = conv + cb[:, s * 3 * W:(s + 1) * 3 * W] * cw_ref[s:s + 1, :]
    cnew_ref[:, 0:6 * W] = cb[:, 3 * W:9 * W]
    cnew_ref[:, 6 * W:9 * W] = raw
    qkv = _silu(conv)
    zg = zg_ref[...]
    g_all = -jnp.exp(alog_ref[...]) * jax.nn.softplus(zg + dtb_ref[...])
    beta_all = jax.nn.sigmoid(zg)
    a_all = jnp.exp(g_all)
    r = lax.broadcasted_iota(jnp.int32, (LANE, LANE), 0)
    c = lax.broadcasted_iota(jnp.int32, (LANE, LANE), 1)
    eye = r == c

    def col(row):
        return jnp.sum(jnp.where(eye, jnp.broadcast_to(row, (LANE, LANE)), 0.0), axis=1, keepdims=True)

    for h in range(n_heads):
        ls = slice(h * LANE, (h + 1) * LANE)
        vh = vg[:, ls]
        mu = jnp.mean(vh, axis=-1, keepdims=True)
        d = vh - mu
        var = jnp.mean(d * d, axis=-1, keepdims=True)
        vn = d * lax.rsqrt(var + EPS) * lng_ref[:, ls] + lnb_ref[:, ls]
        vrow_ref[:, ls] = vn
        ya_ref[:, ls] = u[:, ls] * (vn * w00_ref[:, ls] + b0_ref[:, ls])
        q = qkv[:, h * LANE:(h + 1) * LANE]
        k = qkv[:, W + h * LANE:W + (h + 1) * LANE]
        v = qkv[:, 2 * W + h * LANE:2 * W + (h + 1) * LANE]
        q = q * lax.rsqrt(jnp.sum(q * q, axis=-1, keepdims=True) + EPS) * (LANE ** -0.5)
        k = k * lax.rsqrt(jnp.sum(k * k, axis=-1, keepdims=True) + EPS)
        for j in range(nb):
            kc = col(k[j:j + 1])
            qc = col(q[j:j + 1])
            Sd = s0_ref[j, h] * a_all[j:j + 1, h:h + 1]
            kS = jnp.sum(kc * Sd, axis=0, keepdims=True)
            delta = (v[j:j + 1] - kS) * beta_all[j:j + 1, n_heads + h:n_heads + h + 1]
            Sn = Sd + kc * delta
            snew_ref[j, h] = Sn
            o = jnp.sum(qc * Sn, axis=0, keepdims=True)
            on = o * lax.rsqrt(jnp.mean(o * o, axis=-1, keepdims=True) + EPS) * og_ref[...]
            ob_ref[j:j + 1, ls] = on * _silu(gz_ref[j:j + 1, ls])


def _even_step(z, z_tail, conv_buf, s0, ln_g, ln_b, ws, bs, cw, a_log, dtb, og, *, nb):
    Bs = z.shape[0]
    H = a_log.shape[0]
    W = H * LANE
    pad = lambda t: jnp.zeros((1, LANE), F32).at[0, :H].set(t)
    w00 = jnp.repeat(ws[:, 0, 0], LANE).reshape(1, W)
    b0 = jnp.repeat(bs[:, 0], LANE).reshape(1, W)
    cb = conv_buf.reshape(Bs, (CONV_W - 1) * 3 * W)
    zspec = lambda j: pl.BlockSpec((nb, W), lambda i: (i, j))
    vec = lambda n: pl.BlockSpec((1, n), lambda i: (0, 0))
    ya, vrow, ob, cnew, snew = pl.pallas_call(
        functools.partial(_even_step_kernel, n_heads=H),
        grid=(Bs // nb,),
        in_specs=[zspec(0), zspec(1), zspec(2), zspec(3), zspec(4), zspec(5),
                  pl.BlockSpec((nb, LANE), lambda i: (i, 0)),
                  pl.BlockSpec((nb, 9 * W), lambda i: (i, 0)),
                  pl.BlockSpec((nb, H, LANE, LANE), lambda i: (i, 0, 0, 0)),
                  vec(W), vec(W), vec(W), vec(W),
                  pl.BlockSpec((CONV_W, 3 * W), lambda i: (0, 0)),
                  vec(LANE), vec(LANE), vec(LANE)],
        out_specs=[pl.BlockSpec((nb, W), lambda i: (i, 0)),
                   pl.BlockSpec((nb, W), lambda i: (i, 0)),
                   pl.BlockSpec((nb, W), lambda i: (i, 0)),
                   pl.BlockSpec((nb, 9 * W), lambda i: (i, 0)),
                   pl.BlockSpec((nb, H, LANE, LANE), lambda i: (i, 0, 0, 0))],
        out_shape=[jax.ShapeDtypeStruct((Bs, W), F32),
                   jax.ShapeDtypeStruct((Bs, W), F32),
                   jax.ShapeDtypeStruct((Bs, W), F32),
                   jax.ShapeDtypeStruct((Bs, 9 * W), F32),
                   jax.ShapeDtypeStruct((Bs, H, LANE, LANE), F32)],
        compiler_params=_params(1),
        name="even_step_sample",
    )(z, z, z, z, z, z, z_tail, cb, s0, ln_g.reshape(1, W), ln_b.reshape(1, W), w00, b0, cw,
      pad(a_log), pad(dtb), og.reshape(1, LANE))
    return ya, vrow, ob, cnew.reshape(Bs, CONV_W - 1, 3 * W), snew


def _logf_kernel(f_ref, bf_ref, lf_ref, cum_ref, carry):
    @pl.when(pl.program_id(1) == 0)
    def _():
        carry[...] = jnp.zeros(carry.shape, F32)

    tm = f_ref.shape[0]
    lf = jax.nn.log_sigmoid(f_ref[...] + bf_ref[...])
    lf_ref[...] = lf
    r = lax.broadcasted_iota(jnp.int32, (tm, tm), 0)
    c = lax.broadcasted_iota(jnp.int32, (tm, tm), 1)
    tril_bf = jnp.where(r >= c, 1.0, 0.0).astype(BF16)
    cum = _dot_sel(tril_bf, lf) + carry[...]
    cum_ref[...] = cum
    carry[...] = cum[tm - 1:tm, :]


def _logf_cumsum(z_tail, b_f_pad, *, batch, seq, tm):
    nblk = seq // tm
    return pl.pallas_call(
        _logf_kernel,
        grid=(batch, nblk),
        in_specs=[pl.BlockSpec((tm, LANE), lambda b, i: (b * nblk + i, 0)),
                  pl.BlockSpec((1, LANE), lambda b, i: (0, 0))],
        out_specs=[pl.BlockSpec((tm, LANE), lambda b, i: (b * nblk + i, 0)),
                   pl.BlockSpec((tm, LANE), lambda b, i: (b * nblk + i, 0))],
        out_shape=[jax.ShapeDtypeStruct((batch * seq, LANE), F32),
                   jax.ShapeDtypeStruct((batch * seq, LANE), F32)],
        scratch_shapes=[pltpu.VMEM((1, LANE), F32)],
        compiler_params=_params(2),
        name="logf_cumsum",
    )(z_tail, b_f_pad)


def _pair_norm(x, g, lo, dh):
    x2 = x * x
    sa = jnp.sum(jnp.where(lo, x2, 0.0), axis=-1, keepdims=True)
    sb = jnp.sum(jnp.where(lo, 0.0, x2), axis=-1, keepdims=True)
    ms = jnp.where(lo, sa, sb) * (1.0 / dh)
    return x * lax.rsqrt(ms + EPS) * g


def _fox_kernel(q_ref, k_ref, v_ref, f_ref, qg_ref, kg_ref, o_ref, kn_ref, *scr, tq, dh, precise):
    khi, vhi = scr[0], scr[1]
    klo, vlo = (scr[2], scr[3]) if precise else (None, None)
    i = pl.program_id(2)
    S = k_ref.shape[1]
    lo = lax.broadcasted_iota(jnp.int32, (1, LANE), 1) < dh

    def put(hi_ref, lo_ref, rows, val):
        hi = val.astype(BF16)
        hi_ref[rows, :] = hi
        if precise:
            lo_ref[rows, :] = (val - hi.astype(F32)).astype(BF16)

    @pl.when(i == 0)
    def _():
        for c0 in range(0, S, tq):
            rows = slice(c0, c0 + tq)
            kn = _pair_norm(k_ref[0, rows, :], kg_ref[...], lo, dh)
            kn_ref[0, rows, :] = kn
            put(khi, klo, rows, kn)
            put(vhi, vlo, rows, v_ref[0, rows, :])

    q = _pair_norm(q_ref[0], qg_ref[...], lo, dh) * (dh ** -0.5)
    qs = tuple(_split2(qq) if precise else (qq.astype(BF16), None)
               for qq in (jnp.where(lo, q, 0.0), jnp.where(lo, 0.0, q)))
    rr = lax.broadcasted_iota(jnp.int32, (tq, tq), 0)
    cc = lax.broadcasted_iota(jnp.int32, (tq, tq), 1)
    causal = cc <= rr

    def chunk(j, carry, masked):
        rows = pl.ds(pl.multiple_of(j * tq, tq), tq)
        kh, vh = khi[rows, :], vhi[rows, :]
        out = []
        for hh in range(2):
            m, l, acc = carry[hh]
            fj = f_ref[0, 0, hh, pl.ds(j, 1), :]
            qh, ql = qs[hh]
            s = _dot_nt(qh, kh)
            if precise:
                s = s + (_dot_nt(qh, klo[rows, :]) + _dot_nt(ql, kh))
            s = s - fj
            if masked:
                s = jnp.where(causal, s, NEG)
            m_new = jnp.maximum(m, jnp.max(s, axis=-1, keepdims=True))
            alpha = jnp.exp(m - m_new)
            p = jnp.exp(s - m_new)
            l = alpha * l + jnp.sum(p, axis=-1, keepdims=True)
            ph = p.astype(BF16)
            pv = _dot(ph, vh)
            if precise:
                pl_ = (p - ph.astype(F32)).astype(BF16)
                pv = pv + (_dot(ph, vlo[rows, :]) + _dot(pl_, vh))
            acc = alpha * acc + pv
            out.append((m_new, l, acc))
        return tuple(out)

    init = tuple((jnp.full((tq, 1), NEG, F32), jnp.zeros((tq, 1), F32), jnp.zeros((tq, LANE), F32))
                 for _ in range(2))
    carry = lax.fori_loop(0, i, lambda j, cr: chunk(j, cr, False), init)
    (ma, la, acca), (mb, lb, accb) = chunk(i, carry, True)
    o_ref[0] = jnp.where(lo, acca / la, accb / lb).astype(o_ref.dtype)


def _fox_prompt(z3, fcum, qg, kg, *, tq, precise):
    B, S, N3 = z3.shape
    dh = qg.shape[0]
    H = N3 // (3 * dh)
    HP = H // 2
    nq = S // tq
    f5 = fcum.reshape(B, HP, 2, nq, tq)
    g2 = lambda g: jnp.concatenate([g, g]).reshape(1, LANE)
    return pl.pallas_call(
        functools.partial(_fox_kernel, tq=tq, dh=dh, precise=precise),
        grid=(B, HP, nq),
        in_specs=[pl.BlockSpec((1, tq, LANE), lambda b, p, i: (b, i, p)),
                  pl.BlockSpec((1, S, LANE), lambda b, p, i: (b, 0, HP + p)),
                  pl.BlockSpec((1, S, LANE), lambda b, p, i: (b, 0, 2 * HP + p)),
                  pl.BlockSpec((1, 1, 2, nq, tq), lambda b, p, i: (b, p, 0, 0, 0)),
                  pl.BlockSpec((1, LANE), lambda b, p, i: (0, 0)),
                  pl.BlockSpec((1, LANE), lambda b, p, i: (0, 0))],
        out_specs=[pl.BlockSpec((1, tq, LANE), lambda b, p, i: (b, i, p)),
                   pl.BlockSpec((1, S, LANE), lambda b, p, i: (b, 0, p))],
        out_shape=[jax.ShapeDtypeStruct((B, S, H * dh), F32 if precise else BF16),
                   jax.ShapeDtypeStruct((B, S, H * dh), F32)],
        scratch_shapes=[pltpu.VMEM((S, LANE), BF16), pltpu.VMEM((S, LANE), BF16)] * (2 if precise else 1),
        compiler_params=_params(3),
        name="fox_attention_prompt",
    )(z3, z3, z3, f5, g2(qg), g2(kg))


def _head_sel(n_heads, dh):
    W = n_heads * dh
    col_head = jnp.arange(W, dtype=jnp.int32) // dh
    G = (col_head[:, None] == jnp.arange(LANE, dtype=jnp.int32)[None, :]).astype(BF16)
    return G, G.T


def _qknorm_kernel(q_ref, k_ref, f_ref, qg_ref, kg_ref, bf_ref, g_ref, e_ref, qo_ref, ko_ref, lf_ref, *, dh):
    G = g_ref[...]
    E = e_ref[...]

    def norm(x, gain):
        ms = _x_dot_sel(x * x, G) * (1.0 / dh)
        return x * _x_dot_sel(lax.rsqrt(ms + EPS), E) * gain

    qo_ref[...] = norm(q_ref[...], qg_ref[...]) * (dh ** -0.5)
    ko_ref[...] = norm(k_ref[...], kg_ref[...])
    lf_ref[...] = jax.nn.log_sigmoid(f_ref[...] + bf_ref[...])


def _qknorm_sample(z, z_tail, qg, kg, b_f_pad, G, E, *, n_heads):
    Bs = z.shape[0]
    dh = qg.shape[0]
    W = n_heads * dh
    tile = lambda g: jnp.tile(g, n_heads).reshape(1, W)
    full = lambda a: pl.BlockSpec(a.shape, lambda i: (0,) * a.ndim)
    return pl.pallas_call(
        functools.partial(_qknorm_kernel, dh=dh),
        grid=(1,),
        in_specs=[pl.BlockSpec((Bs, W), lambda i: (0, 0)),
                  pl.BlockSpec((Bs, W), lambda i: (0, 1)),
                  full(z_tail), pl.BlockSpec((1, W), lambda i: (0, 0)), pl.BlockSpec((1, W), lambda i: (0, 0)),
                  full(b_f_pad), full(G), full(E)],
        out_specs=[pl.BlockSpec((Bs, W), lambda i: (0, 0)),
                   pl.BlockSpec((Bs, W), lambda i: (0, 0)),
                   pl.BlockSpec((Bs, LANE), lambda i: (0, 0))],
        out_shape=[jax.ShapeDtypeStruct((Bs, W), F32), jax.ShapeDtypeStruct((Bs, W), F32),
                   jax.ShapeDtypeStruct((Bs, LANE), F32)],
        compiler_params=_params(1),
        name="qknorm_sample",
    )(z, z, z_tail, tile(qg), tile(kg), b_f_pad, G, E)


def _decode_kernel(pt_ref, q_ref, kn_ref, vn_ref, lfn_ref, g_ref, *rest, pages, n_heads, dh):
    H = n_heads
    k_refs = rest[:pages]
    v_refs = rest[pages:2 * pages]
    lf_refs = rest[2 * pages:3 * pages]
    o_ref = rest[3 * pages]
    qb, acc, m_scr, l_scr, c_scr, s_scr, p_scr, a_scr = rest[3 * pages + 1:]
    j = pl.program_id(1)
    P = k_refs[0].shape[4]
    lo = lax.broadcasted_iota(jnp.int32, (1, LANE), 1) < dh

    def lane_col(row):
        return jnp.broadcast_to(row, (LANE, LANE)).T

    @pl.when(j == 0)
    def _():
        q = q_ref[0]
        for hp in range(H // 2):
            t = lane_col(q[:, hp * LANE:(hp + 1) * LANE])
            qb[2 * hp] = t[0:dh]
            qb[2 * hp + 1] = t[dh:2 * dh]
        m_scr[...] = jnp.full(m_scr.shape, NEG, F32)
        l_scr[...] = jnp.zeros(l_scr.shape, F32)
        c_scr[...] = jnp.zeros(c_scr.shape, F32)
        acc[...] = jnp.zeros(acc.shape, F32)

    r = lax.broadcasted_iota(jnp.int32, (P, P), 0)
    c = lax.broadcasted_iota(jnp.int32, (P, P), 1)
    triu_bf = jnp.where(r <= c, 1.0, 0.0).astype(BF16)

    for p in range(pages):
        for h in range(H):
            s_scr[h:h + 1, :] = jnp.sum(qb[h] * k_refs[p][0, 0, h], axis=0, keepdims=True)
        fcum = _x_dot_sel(lf_refs[p][0, 0], triu_bf) + c_scr[...]
        c_scr[...] = jnp.broadcast_to(fcum[:, P - 1:P], (H, P))
        s2 = s_scr[...] - fcum
        m_old = m_scr[...]
        m_new = jnp.maximum(m_old, jnp.max(s2, axis=1, keepdims=True))
        alpha = jnp.exp(m_old - m_new)
        pr = jnp.exp(s2 - m_new)
        l_scr[...] = alpha * l_scr[...] + jnp.sum(pr, axis=1, keepdims=True)
        m_scr[...] = m_new
        p_scr[...] = pr
        a_scr[...] = alpha
        for h in range(H):
            acc[h] = acc[h] * a_scr[h:h + 1, :] + p_scr[h:h + 1, :] * v_refs[p][0, 0, h]

    @pl.when(j == pl.num_programs(1) - 1)
    def _():
        q = q_ref[0]
        vn = vn_ref[0]
        prod = jnp.broadcast_to(q * kn_ref[0], (8, q.shape[1]))
        ph, plo = _split2(prod)
        s_row = (_dot(ph, g_ref[...]) + _dot(plo, g_ref[...]))[0:1, :]
        s2 = lane_col(s_row)[0:H] - (c_scr[...] + lane_col(lfn_ref[0])[0:H])
        m_old = m_scr[...]
        m_new = jnp.maximum(m_old, s2)
        alpha = jnp.exp(m_old - m_new)
        pn = jnp.exp(s2 - m_new)
        l = alpha * l_scr[...] + pn
        for hp in range(H // 2):
            both = jnp.concatenate([acc[2 * hp], acc[2 * hp + 1]], axis=0)
            past = jnp.sum(both.T, axis=0, keepdims=True)
            pick = lambda x: jnp.where(lo, x[2 * hp:2 * hp + 1, :], x[2 * hp + 1:2 * hp + 2, :])
            o = (past * pick(alpha) + pick(pn) * vn[:, hp * LANE:(hp + 1) * LANE]) / pick(l)
            o_ref[0, :, hp * LANE:(hp + 1) * LANE] = o


def _decode_attention(page_table, q, kn, vn, lfn, cache_kt, cache_vt, cache_lft, G, *, layer, n_heads):
    Bs, W = q.shape
    npg = page_table.shape[1]
    H, dh, P = cache_kt.shape[2:]
    pp = PAGES_PER_STEP
    pt = page_table.reshape(-1)

    def page(p):
        return lambda b, j, pt_ref: (layer, pt_ref[b * npg + j * pp + p], 0, 0, 0)

    def page4(p):
        return lambda b, j, pt_ref: (layer, pt_ref[b * npg + j * pp + p], 0, 0)

    row = lambda n: pl.BlockSpec((1, 1, n), lambda b, j, pt_ref: (b, 0, 0))
    in_specs = [row(W), row(W), row(W), row(LANE),
                pl.BlockSpec(G.shape, lambda b, j, pt_ref: (0, 0))]
    in_specs += [pl.BlockSpec((1, 1, H, dh, P), page(p)) for p in range(pp)]
    in_specs += [pl.BlockSpec((1, 1, H, dh, P), page(p)) for p in range(pp)]
    in_specs += [pl.BlockSpec((1, 1, H, P), page4(p)) for p in range(pp)]
    r3 = lambda a: a.reshape(Bs, 1, a.shape[1])
    hp_tile = lambda: pltpu.VMEM((H, P), F32)
    out = pl.pallas_call(
        functools.partial(_decode_kernel, pages=pp, n_heads=n_heads, dh=dh),
        grid_spec=pltpu.PrefetchScalarGridSpec(
            num_scalar_prefetch=1,
            grid=(Bs, npg // pp),
            in_specs=in_specs,
            out_specs=pl.BlockSpec((1, 1, W), lambda b, j, pt_ref: (b, 0, 0)),
            scratch_shapes=[pltpu.VMEM((H, dh, P), F32), pltpu.VMEM((H, dh, P), F32),
                            hp_tile(), hp_tile(), hp_tile(), hp_tile(), hp_tile(), hp_tile()]),
        out_shape=jax.ShapeDtypeStruct((Bs, 1, W), F32),
        compiler_params=_params(2),
        name="fox_decode_paged",
    )(pt, r3(q), r3(kn), r3(vn), r3(lfn), G,
      *([cache_kt] * pp), *([cache_vt] * pp), *([cache_lft] * pp))
    return out.reshape(Bs, W)


def _router_kernel(x_ref, g_ref, sc_ref, sh_ref, wr_ref, br_ref, h_ref, lg_ref):
    h = _rms_mod(x_ref[...], g_ref[...], sc_ref[0], sh_ref[0])
    h_ref[...] = h
    lg_ref[...] = _dot3(h, wr_ref[...]) + br_ref[...]


def _router(x, g, sc, sh, wr_pad, br_pad, *, tm, rows_per_mod):
    M, D = x.shape
    bpm = rows_per_mod // tm
    R = sc.shape[1]
    mod = pl.BlockSpec((1, R, D), lambda i: (i // bpm, 0, 0))
    return pl.pallas_call(
        _router_kernel,
        grid=(M // tm,),
        in_specs=[pl.BlockSpec((tm, D), lambda i: (i, 0)),
                  pl.BlockSpec((1, D), lambda i: (0, 0)),
                  mod, mod,
                  pl.BlockSpec((D, LANE), lambda i: (0, 0)),
                  pl.BlockSpec((1, LANE), lambda i: (0, 0))],
        out_specs=[pl.BlockSpec((tm, D), lambda i: (i, 0)),
                   pl.BlockSpec((tm, LANE), lambda i: (i, 0))],
        out_shape=[jax.ShapeDtypeStruct((M, D), F32), jax.ShapeDtypeStruct((M, LANE), F32)],
        compiler_params=_params(1),
        name="router",
    )(x, g, sc, sh, wr_pad, br_pad)


def _moe_kernel(be_ref, na_ref, x_ref, w1_ref, b1_ref, w2_ref, b2_ref, o_ref, *scr, f_chunk, precise):
    w1hi, w2hi = scr[0], scr[1]
    w1lo, w2lo = (scr[2], scr[3]) if precise else (None, None)
    i = pl.program_id(0)
    active = i < na_ref[0]
    e = be_ref[i]
    prev = be_ref[jnp.maximum(i - 1, 0)]
    first = jnp.logical_or(i == 0, e != prev)
    FF = w2hi.shape[0]

    @pl.when(jnp.logical_and(active, first))
    def _():
        _cast_rows(lambda sl: w1_ref[0, 0, sl, :], w1hi, w1lo)
        _cast_rows(lambda sl: w2_ref[0, 0, sl, :], w2hi, w2lo)

    @pl.when(active)
    def _():
        x = x_ref[...]
        xh = x.astype(BF16)
        xl = (x - xh.astype(F32)).astype(BF16) if precise else None

        def mm1(cols):
            out = _dot(xh, w1hi[:, cols])
            if precise:
                out = out + (_dot(xh, w1lo[:, cols]) + _dot(xl, w1hi[:, cols]))
            return out

        acc = None
        for f0 in range(0, FF, f_chunk):
            glu = mm1(slice(f0, f0 + f_chunk)) + b1_ref[0, 0, :, f0:f0 + f_chunk]
            lin = mm1(slice(FF + f0, FF + f0 + f_chunk)) + b1_ref[0, 0, :, FF + f0:FF + f0 + f_chunk]
            glu = jnp.minimum(glu, SWIGLU_LIMIT)
            lin = jnp.clip(lin, -SWIGLU_LIMIT, SWIGLU_LIMIT)
            a = glu * jax.nn.sigmoid(SWIGLU_ALPHA * glu) * (lin + 1.0)
            t = _mm_w(a, w2hi[f0:f0 + f_chunk, :], w2lo[f0:f0 + f_chunk, :] if precise else None)
            acc = t if acc is None else acc + t
        o_ref[...] = acc + b2_ref[0, 0]

    @pl.when(jnp.logical_not(active))
    def _():
        o_ref[...] = jnp.zeros(o_ref.shape, F32)


def _moe_experts(blk_e, n_active, xb, w1, b1, w2, b2, *, layer, bm, precise):
    NS = xb.shape[0]
    L, E, D, F2 = w1.shape
    FF = F2 // 2
    nb = NS // bm
    return pl.pallas_call(
        functools.partial(_moe_kernel, f_chunk=512, precise=precise),
        grid_spec=pltpu.PrefetchScalarGridSpec(
            num_scalar_prefetch=2,
            grid=(nb,),
            in_specs=[pl.BlockSpec((bm, D), lambda i, be, na: (i, 0)),
                      pl.BlockSpec((1, 1, D, F2), lambda i, be, na: (layer, be[i], 0, 0)),
                      pl.BlockSpec((1, 1, 1, F2), lambda i, be, na: (layer, be[i], 0, 0)),
                      pl.BlockSpec((1, 1, FF, D), lambda i, be, na: (layer, be[i], 0, 0)),
                      pl.BlockSpec((1, 1, 1, D), lambda i, be, na: (layer, be[i], 0, 0))],
            out_specs=pl.BlockSpec((bm, D), lambda i, be, na: (i, 0)),
            scratch_shapes=[pltpu.VMEM((D, F2), BF16), pltpu.VMEM((FF, D), BF16)] * (2 if precise else 1)),
        out_shape=jax.ShapeDtypeStruct((NS, D), F32),
        compiler_params=_params(1),
        name="moe_experts",
    )(blk_e, n_active, xb, w1, b1.reshape(L, E, 1, F2), w2, b2.reshape(L, E, 1, D))


def _moe_ffn(h_all, logits, w1, b1, w2, b2, *, layer, precise):
    T = h_all.shape[0]
    D = w2.shape[3]
    bm = MOE_TILE
    top_v, top_i = lax.top_k(logits, TOP_K)
    gate = jax.nn.softmax(top_v, axis=-1)
    n = T * TOP_K
    e_km = top_i.T.reshape(-1).astype(jnp.int32)
    t_km = jnp.tile(jnp.arange(T, dtype=jnp.int32), TOP_K)
    experts = jnp.arange(N_EXPERTS, dtype=jnp.int32)
    onehot = (e_km[:, None] == experts[None, :]).astype(jnp.int32)
    csum = jnp.cumsum(onehot, axis=0)
    counts = csum[-1]
    nblk_e = (counts + bm - 1) // bm
    blk_end = jnp.cumsum(nblk_e)
    blk_start = blk_end - nblk_e
    dest = jnp.sum(onehot * (csum - 1 + (blk_start * bm)[None, :]), axis=1)
    nb = -(-(n + N_EXPERTS * (bm - 1)) // bm)
    n_active = blk_end[-1]
    bidx = jnp.minimum(jnp.arange(nb, dtype=jnp.int32), n_active - 1)
    blk_e = jnp.minimum(jnp.sum((blk_end[None, :] <= bidx[:, None]).astype(jnp.int32), axis=1), N_EXPERTS - 1)
    slot_tok = jnp.zeros((nb * bm,), jnp.int32).at[dest].set(t_km, mode="promise_in_bounds", unique_indices=True)
    xb = h_all.at[slot_tok].get(mode="promise_in_bounds")
    yb = _moe_experts(blk_e, n_active.reshape(1).astype(jnp.int32), xb, w1, b1, w2, b2, layer=layer, bm=bm,
                      precise=precise)
    ya = yb.at[dest].get(mode="promise_in_bounds").reshape(TOP_K, T, D)
    return jnp.sum(ya * gate.T[:, :, None], axis=0)


def kernel(x_prompt, x_sample, cache_k, cache_v, cache_logf, state_delta, state_conv, page_table, c_prompt, c_sample, norm_g, w_ada, b_ada, w_in_even, a_ln_g, a_ln_b, a_ws, a_bs, conv_w, A_log, dt_bias, onorm_g, w_out_even, w_in_odd, b_f, q_norm_g, k_norm_g, w_out_odd, w_router, b_router, w_mlp1, b_mlp1, w_mlp2, b_mlp2):
    Bp, Sp, D = x_prompt.shape
    Bs = x_sample.shape[0]
    depth = norm_g.shape[0]
    Mp = Bp * Sp
    a_heads = a_ws.shape[1]
    a_width = a_heads * LANE
    b_heads = A_log.shape[1]
    b_width = b_heads * LANE
    c_heads = b_f.shape[1]
    c_dh = q_norm_g.shape[1]
    c_width = c_heads * c_dh
    tm = ROW_TILE

    xp = x_prompt.reshape(Mp, D)
    xs = x_sample.reshape(Bs, D)
    mods = _ada_all(jnp.concatenate([c_prompt, c_sample], axis=0), w_ada, b_ada)
    ckt = jnp.transpose(cache_k, (0, 1, 3, 4, 2))
    cvt = jnp.transpose(cache_v, (0, 1, 3, 4, 2))
    clft = jnp.transpose(cache_logf, (0, 1, 3, 2))
    G, E = _head_sel(c_heads, c_dh)

    def pad_cols(w):
        return jnp.pad(w, ((0, 0), (0, LANE - w.shape[1])))

    kp_l, vp_l, fp_l, ks_l, vs_l, fs_l = [], [], [], [], [], []
    dp_l, cp_l, ds_l, cs_l, av_l = [], [], [], [], []

    for l in range(depth):
        i = l // 2
        pm = l < PRECISE_MIXER_LAYERS
        m6 = mods[l].reshape(Bp + Bs, 6, D)
        mp = [m6[:Bp, j].reshape(Bp, 1, D) for j in range(6)]
        ms = [m6[Bp:, j].reshape(1, Bs, D) for j in range(6)]
        g1 = norm_g[l, 0].reshape(1, D)
        g2 = norm_g[l, 1].reshape(1, D)
        if l % 2 == 0:
            n_main = 2 * a_width + 4 * b_width
            w_main = w_in_even[i][:, :n_main]
            w_tail = pad_cols(w_in_even[i][:, n_main:])
            zp, ztp = _proj(xp, g1, mp[1], mp[0], w_main, w_tail, tm=tm, rows_per_mod=Sp, precise=pm)
            zs, zts = _proj(xs, g1, ms[1], ms[0], w_main, w_tail, tm=Bs, rows_per_mod=Bs, precise=True)
            ya_p = _gmlp(zp, a_ln_g[i], a_ln_b[i], a_ws[i], a_bs[i], tm=2 * CHUNK, precise=pm)
            ob_p, sfin_p = _delta_chunked(zp, ztp, conv_w[i], A_log[i], dt_bias[i], onorm_g[i],
                                          batch=Bp, seq=Sp, tm=4 * DN_CHUNK, precise=pm)
            ya_s, vrow_s, ob_s, cnew_s, snew_s = _even_step(
                zs, zts, state_conv[i], state_delta[i], a_ln_g[i], a_ln_b[i], a_ws[i], a_bs[i],
                conv_w[i], A_log[i], dt_bias[i], onorm_g[i], nb=8)
            xp = _outproj([ya_p, ob_p], w_out_even[i], xp, mp[2], tm=tm, rows_per_mod=Sp, precise=pm)
            xs = _outproj([ya_s, ob_s], w_out_even[i], xs, ms[2], tm=Bs, rows_per_mod=Bs, precise=True)
            qkv_raw = zp[:, 2 * a_width:2 * a_width + 3 * b_width].reshape(Bp, Sp, 3 * b_width)
            dp_l.append(sfin_p)
            cp_l.append(qkv_raw[:, Sp - (CONV_W - 1):])
            ds_l.append(snew_s)
            cs_l.append(cnew_s)
            av_l.append(vrow_s.reshape(Bs, 1, a_width))
        else:
            n_main = 3 * c_width
            w_main = w_in_odd[i][:, :n_main]
            w_tail = pad_cols(w_in_odd[i][:, n_main:])
            bf_pad = jnp.zeros((1, LANE), F32).at[0, :c_heads].set(b_f[i])
            zp, ztp = _proj(xp, g1, mp[1], mp[0], w_main, w_tail, tm=tm, rows_per_mod=Sp, precise=pm)
            zs, zts = _proj(xs, g1, ms[1], ms[0], w_main, w_tail, tm=Bs, rows_per_mod=Bs, precise=True)
            lf_p, fc_p = _logf_cumsum(ztp, bf_pad, batch=Bp, seq=Sp, tm=512)
            lf_p = lf_p[:, :c_heads].reshape(Bp, Sp, c_heads)
            fcum = jnp.swapaxes(fc_p[:, :c_heads].reshape(Bp, Sp, c_heads), 1, 2)
            o_p, kn_p = _fox_prompt(zp.reshape(Bp, Sp, n_main), fcum, q_norm_g[i], k_norm_g[i], tq=512, precise=pm)
            xp = _outproj([o_p.reshape(Mp, c_width)], w_out_odd[i], xp, mp[2], tm=tm, rows_per_mod=Sp, precise=pm)
            q_s, k_s, lf_s = _qknorm_sample(zs, zts, q_norm_g[i], k_norm_g[i], bf_pad, G, E, n_heads=c_heads)
            v_s = zs[:, 2 * c_width:]
            o_s = _decode_attention(page_table, q_s, k_s, v_s, lf_s, ckt, cvt, clft, G,
                                    layer=i, n_heads=c_heads)
            xs = _outproj([o_s], w_out_odd[i], xs, ms[2], tm=Bs, rows_per_mod=Bs, precise=True)
            kp_l.append(kn_p.reshape(Bp, Sp, c_heads, c_dh))
            vp_l.append(zp[:, 2 * c_width:].reshape(Bp, Sp, c_heads, c_dh))
            fp_l.append(lf_p)
            ks_l.append(k_s.reshape(Bs, 1, c_heads, c_dh))
            vs_l.append(v_s.reshape(Bs, 1, c_heads, c_dh))
            fs_l.append(lf_s[:, :c_heads].reshape(Bs, 1, c_heads))
        wr_pad = pad_cols(w_router[l])
        br_pad = jnp.zeros((1, LANE), F32).at[0, :N_EXPERTS].set(b_router[l])
        hp, lgp = _router(xp, g2, mp[4], mp[3], wr_pad, br_pad, tm=tm, rows_per_mod=Sp)
        hs, lgs = _router(xs, g2, ms[4], ms[3], wr_pad, br_pad, tm=Bs, rows_per_mod=Bs)
        h_all = jnp.concatenate([hp, hs], axis=0)
        lg_all = jnp.concatenate([lgp, lgs], axis=0)[:, :N_EXPERTS]
        y_all = _moe_ffn(h_all, lg_all, w_mlp1, b_mlp1, w_mlp2, b_mlp2, layer=l, precise=l < PRECISE_MOE_LAYERS)
        xp = xp + (mp[5] * y_all[:Mp].reshape(Bp, Sp, D)).reshape(Mp, D)
        xs = xs + ms[5][0] * y_all[Mp:]

    return (xp.reshape(Bp, Sp, D), xs.reshape(Bs, 1, D),
            jnp.stack(kp_l), jnp.stack(vp_l), jnp.stack(fp_l),
            jnp.stack(ks_l), jnp.stack(vs_l), jnp.stack(fs_l),
            jnp.stack(dp_l), jnp.stack(cp_l), jnp.stack(ds_l), jnp.stack(cs_l), jnp.stack(av_l))
```

```python
import functools

import jax
import jax.numpy as jnp
from jax import lax
from jax.experimental import pallas as pl
from jax.experimental.pallas import tpu as pltpu

F32 = jnp.float32
BF16 = jnp.bfloat16

EPS = 1e-6
CHUNK = 128
DN_CHUNK = 64
N_EXPERTS = 32
TOP_K = 4
SWIGLU_ALPHA = 1.702
SWIGLU_LIMIT = 7.0
CONV_W = 4
LANE = 128
NEG = -1e30

VMEM_LIMIT = 56 * 1024 * 1024
ROW_TILE = 512
MOE_TILE = 512
PAGES_PER_STEP = 8
PRECISE_MIXER_LAYERS = 2
PRECISE_MOE_LAYERS = 1


def _params(n_axes):
    return pltpu.CompilerParams(dimension_semantics=("arbitrary",) * n_axes,
                                vmem_limit_bytes=VMEM_LIMIT)


def _dot(a, b):
    return jnp.dot(a, b, preferred_element_type=F32)


def _dot_nt(a, b):
    return lax.dot_general(a, b, (((1,), (1,)), ((), ())), preferred_element_type=F32)


def _dot_tn(a, b):
    return lax.dot_general(a, b, (((0,), (0,)), ((), ())), preferred_element_type=F32)


def _split2(a):
    hi = a.astype(BF16)
    lo = (a - hi.astype(F32)).astype(BF16)
    return hi, lo


def _split3(a):
    hi = a.astype(BF16)
    r = a - hi.astype(F32)
    mid = r.astype(BF16)
    lo = (r - mid.astype(F32)).astype(BF16)
    return hi, mid, lo


def _dot3(a, b):
    ah, al = _split2(a)
    bh, bl = _split2(b)
    return _dot(ah, bh) + (_dot(ah, bl) + _dot(al, bh))


def _mm(a, b, precise, dot=_dot):
    if precise:
        ah, al = _split2(a)
        bh, bl = _split2(b)
        return dot(ah, bh) + (dot(ah, bl) + dot(al, bh))
    return dot(a.astype(BF16), b.astype(BF16))


def _mm_w(a, w_hi, w_lo):
    ah = a.astype(BF16)
    out = _dot(ah, w_hi)
    if w_lo is not None:
        al = (a - ah.astype(F32)).astype(BF16)
        out = out + (_dot(ah, w_lo) + _dot(al, w_hi))
    return out


def _dot_sel(sel_bf16, x):
    hi, mid, lo = _split3(x)
    return _dot(sel_bf16, hi) + (_dot(sel_bf16, mid) + _dot(sel_bf16, lo))


def _x_dot_sel(x, sel_bf16):
    hi, mid, lo = _split3(x)
    return _dot(hi, sel_bf16) + (_dot(mid, sel_bf16) + _dot(lo, sel_bf16))


def _rms_mod(x, g, sc, sh):
    ms = jnp.mean(x * x, axis=-1, keepdims=True)
    return (x * lax.rsqrt(ms + EPS) * g) * (1.0 + sc) + sh


def _gelu(x):
    return 0.5 * x * (1.0 + lax.erf(x * 0.7071067811865476))


def _silu(x):
    return x * jax.nn.sigmoid(x)


def _cast_rows(load, hi_ref, lo_ref, rows=128):
    n = hi_ref.shape[0] // rows

    def body(r, c):
        sl = pl.ds(pl.multiple_of(r * rows, rows), rows)
        w = load(sl)
        hi = w.astype(BF16)
        hi_ref[sl, :] = hi
        if lo_ref is not None:
            lo_ref[sl, :] = (w - hi.astype(F32)).astype(BF16)
        return c

    lax.fori_loop(0, n, body, 0)


def _ada_kernel(c_ref, w_ref, b_ref, o_ref):
    c = c_ref[...]
    o_ref[0] = _dot3(_silu(c), w_ref[0]) + b_ref[0]


def _ada_all(c_all, w_ada, b_ada):
    L, D, N = w_ada.shape
    nb = c_all.shape[0]
    tn = 1536
    return pl.pallas_call(
        _ada_kernel,
        grid=(L, N // tn),
        in_specs=[pl.BlockSpec((nb, D), lambda l, j: (0, 0)),
                  pl.BlockSpec((1, D, tn), lambda l, j: (l, 0, j)),
                  pl.BlockSpec((1, 1, tn), lambda l, j: (l, 0, j))],
        out_specs=pl.BlockSpec((1, nb, tn), lambda l, j: (l, 0, j)),
        out_shape=jax.ShapeDtypeStruct((L, nb, N), F32),
        compiler_params=_params(2),
        name="ada_mod",
    )(c_all, w_ada, b_ada.reshape(L, 1, N))


def _proj_kernel(x_ref, g_ref, sc_ref, sh_ref, w_ref, wt_ref, o_ref, ot_ref, *scr, n_chunk, precise):
    whi, wthi = scr[0], scr[1]
    wlo, wtlo = (scr[2], scr[3]) if precise else (None, None)

    @pl.when(pl.program_id(0) == 0)
    def _():
        _cast_rows(lambda sl: w_ref[sl, :], whi, wlo)
        _cast_rows(lambda sl: wt_ref[sl, :], wthi, wtlo)

    h = _rms_mod(x_ref[...], g_ref[...], sc_ref[0], sh_ref[0])
    hh = h.astype(BF16)
    hl = (h - hh.astype(F32)).astype(BF16) if precise else None

    def mm(w_hi, w_lo, cols):
        out = _dot(hh, w_hi[:, cols])
        if precise:
            out = out + (_dot(hh, w_lo[:, cols]) + _dot(hl, w_hi[:, cols]))
        return out

    for n0 in range(0, o_ref.shape[1], n_chunk):
        o_ref[:, n0:n0 + n_chunk] = mm(whi, wlo, slice(n0, n0 + n_chunk))
    ot_ref[...] = mm(wthi, wtlo, slice(0, ot_ref.shape[1]))


def _proj(x, g, sc, sh, w, w_tail, *, tm, rows_per_mod, precise):
    M, D = x.shape
    N = w.shape[1]
    NT = w_tail.shape[1]
    bpm = rows_per_mod // tm
    R = sc.shape[1]
    mod = pl.BlockSpec((1, R, D), lambda i: (i // bpm, 0, 0))
    parts = 2 if precise else 1
    return pl.pallas_call(
        functools.partial(_proj_kernel, n_chunk=512, precise=precise),
        grid=(M // tm,),
        in_specs=[pl.BlockSpec((tm, D), lambda i: (i, 0)),
                  pl.BlockSpec((1, D), lambda i: (0, 0)),
                  mod, mod,
                  pl.BlockSpec((D, N), lambda i: (0, 0), pipeline_mode=pl.Buffered(1)),
                  pl.BlockSpec((D, NT), lambda i: (0, 0), pipeline_mode=pl.Buffered(1))],
        out_specs=[pl.BlockSpec((tm, N), lambda i: (i, 0)),
                   pl.BlockSpec((tm, NT), lambda i: (i, 0))],
        out_shape=[jax.ShapeDtypeStruct((M, N), F32), jax.ShapeDtypeStruct((M, NT), F32)],
        scratch_shapes=[pltpu.VMEM((D, N), BF16), pltpu.VMEM((D, NT), BF16)] * parts,
        compiler_params=_params(1),
        name="norm_mod_proj",
    )(x, g, sc, sh, w, w_tail)


def _outproj_kernel(*refs, n_in, precise):
    ys = refs[:n_in]
    w_ref, x_ref, gate_ref, o_ref, whi = refs[n_in:n_in + 5]
    wlo = refs[n_in + 5] if precise else None

    @pl.when(pl.program_id(0) == 0)
    def _():
        _cast_rows(lambda sl: w_ref[sl, :], whi, wlo)

    acc = None
    k0 = 0
    for y_ref in ys:
        kk = y_ref.shape[1]
        t = _mm_w(y_ref[...], whi[k0:k0 + kk, :], wlo[k0:k0 + kk, :] if precise else None)
        acc = t if acc is None else acc + t
        k0 += kk
    o_ref[...] = x_ref[...] + gate_ref[0] * acc


def _outproj(ys, w, x, gate, *, tm, rows_per_mod, precise):
    M, D = x.shape
    K = w.shape[0]
    bpm = rows_per_mod // tm
    R = gate.shape[1]
    in_specs = [pl.BlockSpec((tm, y.shape[1]), lambda i: (i, 0)) for y in ys]
    in_specs += [pl.BlockSpec((K, D), lambda i: (0, 0), pipeline_mode=pl.Buffered(1)),
                 pl.BlockSpec((tm, D), lambda i: (i, 0)),
                 pl.BlockSpec((1, R, D), lambda i: (i // bpm, 0, 0))]
    return pl.pallas_call(
        functools.partial(_outproj_kernel, n_in=len(ys), precise=precise),
        grid=(M // tm,),
        in_specs=in_specs,
        out_specs=pl.BlockSpec((tm, D), lambda i: (i, 0)),
        out_shape=jax.ShapeDtypeStruct((M, D), F32),
        scratch_shapes=[pltpu.VMEM((K, D), BF16)] * (2 if precise else 1),
        compiler_params=_params(1),
        name="out_proj_residual",
    )(*ys, w, x, gate)


def _gmlp_kernel(u_ref, v_ref, lng_ref, lnb_ref, ws_ref, bst_ref, o_ref, *, n_heads, precise):
    tm = u_ref.shape[0]
    u = _gelu(u_ref[...])
    v = _gelu(v_ref[...])
    r = lax.broadcasted_iota(jnp.int32, (CHUNK, CHUNK), 0)
    c = lax.broadcasted_iota(jnp.int32, (CHUNK, CHUNK), 1)
    tril = r >= c
    for h in range(n_heads):
        ls = slice(h * LANE, (h + 1) * LANE)
        vh = v[:, ls]
        mu = jnp.mean(vh, axis=-1, keepdims=True)
        d = vh - mu
        var = jnp.mean(d * d, axis=-1, keepdims=True)
        vn = d * lax.rsqrt(var + EPS) * lng_ref[:, ls] + lnb_ref[:, ls]
        wm = jnp.where(tril, ws_ref[h], 0.0)
        bias = bst_ref[:, h:h + 1]
        for c0 in range(0, tm, CHUNK):
            mixed = _mm(wm, vn[c0:c0 + CHUNK], precise) + bias
            o_ref[c0:c0 + CHUNK, ls] = (u[c0:c0 + CHUNK, ls] * mixed).astype(o_ref.dtype)


def _gmlp(z, ln_g, ln_b, ws, bs, *, tm, precise):
    M = z.shape[0]
    H = ws.shape[0]
    W = H * LANE
    return pl.pallas_call(
        functools.partial(_gmlp_kernel, n_heads=H, precise=precise),
        grid=(M // tm,),
        in_specs=[pl.BlockSpec((tm, W), lambda i: (i, 0)),
                  pl.BlockSpec((tm, W), lambda i: (i, 1)),
                  pl.BlockSpec((1, W), lambda i: (0, 0)),
                  pl.BlockSpec((1, W), lambda i: (0, 0)),
                  pl.BlockSpec((H, CHUNK, CHUNK), lambda i: (0, 0, 0)),
                  pl.BlockSpec((CHUNK, H), lambda i: (0, 0))],
        out_specs=pl.BlockSpec((tm, W), lambda i: (i, 0)),
        out_shape=jax.ShapeDtypeStruct((M, W), F32 if precise else BF16),
        compiler_params=_params(1),
        name="gmlp_chunk",
    )(z, z, ln_g.reshape(1, W), ln_b.reshape(1, W), ws, bs.T)


def _delta_kernel(zq_ref, zk_ref, zv_ref, zg_ref, gz_ref, cw_ref, alog_ref, dtb_ref, og_ref,
                  o_ref, sfin_ref, xbuf, state, *, n_heads, precise):
    i = pl.program_id(1)
    tm = zq_ref.shape[0]
    H = n_heads
    W = H * LANE
    C = DN_CHUNK
    HC = H * C

    @pl.when(i == 0)
    def _():
        xbuf[0:8, :] = jnp.zeros((8, 3 * W), F32)
        state[...] = jnp.zeros(state.shape, F32)

    xbuf[8:8 + tm, 0:W] = zq_ref[...]
    xbuf[8:8 + tm, W:2 * W] = zk_ref[...]
    xbuf[8:8 + tm, 2 * W:3 * W] = zv_ref[...]
    conv = xbuf[8:8 + tm, :] * cw_ref[3:4, :]
    for s in range(1, CONV_W):
        conv = conv + xbuf[8 - s:8 - s + tm, :] * cw_ref[3 - s:4 - s, :]
    tail = xbuf[tm:tm + 8, :]
    xbuf[0:8, :] = tail
    qkv = _silu(conv)

    zg = zg_ref[...]
    g_all = -jnp.exp(alog_ref[...]) * jax.nn.softplus(zg + dtb_ref[...])
    beta_all = jax.nn.sigmoid(zg)

    r = lax.broadcasted_iota(jnp.int32, (C, C), 0)
    c = lax.broadcasted_iota(jnp.int32, (C, C), 1)
    tril_bf = jnp.where(r >= c, 1.0, 0.0).astype(BF16)
    rr = lax.broadcasted_iota(jnp.int32, (HC, HC), 0)
    cc = lax.broadcasted_iota(jnp.int32, (HC, HC), 1)
    same = (rr // C) == (cc // C)
    incl = jnp.logical_and(same, rr >= cc)
    strict = jnp.logical_and(same, rr > cc)
    eye = rr == cc
    eye_f = jnp.where(eye, 1.0, 0.0)

    chunks = []
    for c0 in range(0, tm, C):
        rows = slice(c0, c0 + C)
        gc_all = _dot_sel(tril_bf, g_all[rows])
        qs, ks, vs, gcs, bts, gls = [], [], [], [], [], []
        for h in range(H):
            q = qkv[rows, h * LANE:(h + 1) * LANE]
            k = qkv[rows, W + h * LANE:W + (h + 1) * LANE]
            qs.append(q * lax.rsqrt(jnp.sum(q * q, axis=-1, keepdims=True) + EPS) * (LANE ** -0.5))
            ks.append(k * lax.rsqrt(jnp.sum(k * k, axis=-1, keepdims=True) + EPS))
            vs.append(qkv[rows, 2 * W + h * LANE:2 * W + (h + 1) * LANE])
            gc = gc_all[:, h:h + 1]
            gcs.append(gc)
            gls.append(jnp.broadcast_to(gc[C - 1:C, :], (C, 1)))
            bts.append(beta_all[rows, H + h:H + h + 1])
        Q = jnp.concatenate(qs, axis=0)
        K = jnp.concatenate(ks, axis=0)
        V = jnp.concatenate(vs, axis=0)
        GC = jnp.concatenate(gcs, axis=0)
        GL = jnp.concatenate(gls, axis=0)
        BT = jnp.concatenate(bts, axis=0)
        gc_row = jnp.sum(jnp.where(eye, jnp.broadcast_to(GC, (HC, HC)), 0.0), axis=0, keepdims=True)
        decay = jnp.exp(jnp.where(incl, GC - gc_row, NEG))
        EG = jnp.exp(GC)
        KB = K * BT
        L = jnp.where(strict, _mm(KB, K, precise, _dot_nt) * decay, 0.0)
        chunks.append(dict(rows=rows, Q=Q, K=K, GC=GC, GL=GL, gls=gls, decay=decay, EG=EG, KB=KB,
                           VB=V * BT, T=eye_f - L, P=_split2(L)))

    for _ in range(5):
        for ch in chunks:
            Ph, Pl = ch["P"]
            P = _dot(Ph, Ph) + (_dot(Ph, Pl) + _dot(Pl, Ph))
            Ph, Pl = _split2(P)
            Th, Tl = _split2(ch["T"])
            ch["T"] = ch["T"] + (_dot(Th, Ph) + (_dot(Th, Pl) + _dot(Tl, Ph)))
            ch["P"] = (Ph, Pl)

    for ch in chunks:
        rows, Q, K, GC, GL, gls = ch["rows"], ch["Q"], ch["K"], ch["GC"], ch["GL"], ch["gls"]
        decay, EG, KB = ch["decay"], ch["EG"], ch["KB"]
        TV = _mm(ch["T"], jnp.concatenate([ch["VB"], KB * EG], axis=1), precise)
        value = TV[:, 0:LANE]
        kcd = TV[:, LANE:2 * LANE]
        QK = jnp.where(incl, _mm(Q, K, precise, _dot_nt) * decay, 0.0)
        QG = Q * EG
        KDEC = K * jnp.exp(GL - GC)
        vnew, ointer = [], []
        for h in range(H):
            hs = slice(h * C, (h + 1) * C)
            St = state[h]
            both = _mm(jnp.concatenate([kcd[hs], QG[hs]], axis=0), St, precise)
            vn = value[hs] - both[0:C]
            vnew.append(vn)
            ointer.append(both[C:2 * C])
            state[h] = St * jnp.exp(gls[h][0:1, :]) + _mm(KDEC[hs], vn, precise, _dot_tn)
        O = jnp.concatenate(ointer, axis=0) + _mm(QK, jnp.concatenate(vnew, axis=0), precise)
        for h in range(H):
            ls = slice(h * LANE, (h + 1) * LANE)
            o = O[h * C:(h + 1) * C]
            on = o * lax.rsqrt(jnp.mean(o * o, axis=-1, keepdims=True) + EPS) * og_ref[...]
            o_ref[rows, ls] = (on * _silu(gz_ref[rows, ls])).astype(o_ref.dtype)

    @pl.when(i == pl.num_programs(1) - 1)
    def _():
        sfin_ref[0] = state[...]


def _delta_chunked(z, z_tail, cw, a_log, dtb, og, *, batch, seq, tm, precise):
    H = a_log.shape[0]
    W = H * LANE
    nblk = seq // tm
    row = lambda b, i: b * nblk + i
    pad = lambda t: jnp.zeros((1, LANE), F32).at[0, :H].set(t)
    return pl.pallas_call(
        functools.partial(_delta_kernel, n_heads=H, precise=precise),
        grid=(batch, nblk),
        in_specs=[pl.BlockSpec((tm, W), lambda b, i: (row(b, i), 2)),
                  pl.BlockSpec((tm, W), lambda b, i: (row(b, i), 3)),
                  pl.BlockSpec((tm, W), lambda b, i: (row(b, i), 4)),
                  pl.BlockSpec((tm, LANE), lambda b, i: (row(b, i), 0)),
                  pl.BlockSpec((tm, W), lambda b, i: (row(b, i), 5)),
                  pl.BlockSpec((CONV_W, 3 * W), lambda b, i: (0, 0)),
                  pl.BlockSpec((1, LANE), lambda b, i: (0, 0)),
                  pl.BlockSpec((1, LANE), lambda b, i: (0, 0)),
                  pl.BlockSpec((1, LANE), lambda b, i: (0, 0))],
        out_specs=[pl.BlockSpec((tm, W), lambda b, i: (row(b, i), 0)),
                   pl.BlockSpec((1, H, LANE, LANE), lambda b, i: (b, 0, 0, 0))],
        out_shape=[jax.ShapeDtypeStruct((batch * seq, W), F32 if precise else BF16),
                   jax.ShapeDtypeStruct((batch, H, LANE, LANE), F32)],
        scratch_shapes=[pltpu.VMEM((tm + 8, 3 * W), F32), pltpu.VMEM((H, LANE, LANE), F32)],
        compiler_params=_params(2),
        name="delta_chunked",
    )(z, z, z, z_tail, z, cw, pad(a_log), pad(dtb), og.reshape(1, LANE))


def _even_step_kernel(zu_ref, zv_ref, zq_ref, zk_ref, zvv_ref, gz_ref, zg_ref, cb_ref, s0_ref,
                      lng_ref, lnb_ref, w00_ref, b0_ref, cw_ref, alog_ref, dtb_ref, og_ref,
                      ya_ref, vrow_ref, ob_ref, cnew_ref, snew_ref, *, n_heads):
    nb = zu_ref.shape[0]
    W = n_heads * LANE
    u = _gelu(zu_ref[...])
    vg = _gelu(zv_ref[...])
    raw = jnp.concatenate([zq_ref[...], zk_ref[...], zvv_ref[...]], axis=1)
    cb = cb_ref[...]
    conv = raw * cw_ref[3:4, :]
    for s in range(CONV_W - 1):
        conv = conv + cb[:, s * 3 * W:(s + 1) * 3 * W] * cw_ref[s:s + 1, :]
    cnew_ref[:, 0:6 * W] = cb[:, 3 * W:9 * W]
    cnew_ref[:, 6 * W:9 * W] = raw
    qkv = _silu(conv)
    zg = zg_ref[...]
    g_all = -jnp.exp(alog_ref[...]) * jax.nn.softplus(zg + dtb_ref[...])
    beta_all = jax.nn.sigmoid(zg)
    a_all = jnp.exp(g_all)
    r = lax.broadcasted_iota(jnp.int32, (LANE, LANE), 0)
    c = lax.broadcasted_iota(jnp.int32, (LANE, LANE), 1)
    eye = r == c

    def col(row):
        return jnp.sum(jnp.where(eye, jnp.broadcast_to(row, (LANE, LANE)), 0.0), axis=1, keepdims=True)

    for h in range(n_heads):
        ls = slice(h * LANE, (h + 1) * LANE)
        vh = vg[:, ls]
        mu = jnp.mean(vh, axis=-1, keepdims=True)
        d = vh - mu
        var = jnp.mean(d * d, axis=-1, keepdims=True)
        vn = d * lax.rsqrt(var + EPS) * lng_ref[:, ls] + lnb_ref[:, ls]
        vrow_ref[:, ls] = vn
        ya_ref[:, ls] = u[:, ls] * (vn * w00_ref[:, ls] + b0_ref[:, ls])
        q = qkv[:, h * LANE:(h + 1) * LANE]
        k = qkv[:, W + h * LANE:W + (h + 1) * LANE]
        v = qkv[:, 2 * W + h * LANE:2 * W + (h + 1) * LANE]
        q = q * lax.rsqrt(jnp.sum(q * q, axis=-1, keepdims=True) + EPS) * (LANE ** -0.5)
        k = k * lax.rsqrt(jnp.sum(k * k, axis=-1, keepdims=True) + EPS)
        for j in range(nb):
            kc = col(k[j:j + 1])
            qc = col(q[j:j + 1])
            Sd = s0_ref[j, h] * a_all[j:j + 1, h:h + 1]
            kS = jnp.sum(kc * Sd, axis=0, keepdims=True)
            delta = (v[j:j + 1] - kS) * beta_all[j:j + 1, n_heads + h:n_heads + h + 1]
            Sn = Sd + kc * delta
            snew_ref[j, h] = Sn
            o = jnp.sum(qc * Sn, axis=0, keepdims=True)
            on = o * lax.rsqrt(jnp.mean(o * o, axis=-1, keepdims=True) + EPS) * og_ref[...]
            ob_ref[j:j + 1, ls] = on * _silu(gz_ref[j:j + 1, ls])


def _even_step(z, z_tail, conv_buf, s0, ln_g, ln_b, ws, bs, cw, a_log, dtb, og, *, nb):
    Bs = z.shape[0]
    H = a_log.shape[0]
    W = H * LANE
    pad = lambda t: jnp.zeros((1, LANE), F32).at[0, :H].set(t)
    w00 = jnp.repeat(ws[:, 0, 0], LANE).reshape(1, W)
    b0 = jnp.repeat(bs[:, 0], LANE).reshape(1, W)
    cb = conv_buf.reshape(Bs, (CONV_W - 1) * 3 * W)
    zspec = lambda j: pl.BlockSpec((nb, W), lambda i: (i, j))
    vec = lambda n: pl.BlockSpec((1, n), lambda i: (0, 0))
    ya, vrow, ob, cnew, snew = pl.pallas_call(
        functools.partial(_even_step_kernel, n_heads=H),
        grid=(Bs // nb,),
        in_specs=[zspec(0), zspec(1), zspec(2), zspec(3), zspec(4), zspec(5),
                  pl.BlockSpec((nb, LANE), lambda i: (i, 0)),
                  pl.BlockSpec((nb, 9 * W), lambda i: (i, 0)),
                  pl.BlockSpec((nb, H, LANE, LANE), lambda i: (i, 0, 0, 0)),
                  vec(W), vec(W), vec(W), vec(W),
                  pl.BlockSpec((CONV_W, 3 * W), lambda i: (0, 0)),
                  vec(LANE), vec(LANE), vec(LANE)],
        out_specs=[pl.BlockSpec((nb, W), lambda i: (i, 0)),
                   pl.BlockSpec((nb, W), lambda i: (i, 0)),
                   pl.BlockSpec((nb, W), lambda i: (i, 0)),
                   pl.BlockSpec((nb, 9 * W), lambda i: (i, 0)),
                   pl.BlockSpec((nb, H, LANE, LANE), lambda i: (i, 0, 0, 0))],
        out_shape=[jax.ShapeDtypeStruct((Bs, W), F32),
                   jax.ShapeDtypeStruct((Bs, W), F32),
                   jax.ShapeDtypeStruct((Bs, W), F32),
                   jax.ShapeDtypeStruct((Bs, 9 * W), F32),
                   jax.ShapeDtypeStruct((Bs, H, LANE, LANE), F32)],
        compiler_params=_params(1),
        name="even_step_sample",
    )(z, z, z, z, z, z, z_tail, cb, s0, ln_g.reshape(1, W), ln_b.reshape(1, W), w00, b0, cw,
      pad(a_log), pad(dtb), og.reshape(1, LANE))
    return ya, vrow, ob, cnew.reshape(Bs, CONV_W - 1, 3 * W), snew


def _logf_kernel(f_ref, bf_ref, lf_ref, cum_ref, carry):
    @pl.when(pl.program_id(1) == 0)
    def _():
        carry[...] = jnp.zeros(carry.shape, F32)

    tm = f_ref.shape[0]
    lf = jax.nn.log_sigmoid(f_ref[...] + bf_ref[...])
    lf_ref[...] = lf
    r = lax.broadcasted_iota(jnp.int32, (tm, tm), 0)
    c = lax.broadcasted_iota(jnp.int32, (tm, tm), 1)
    tril_bf = jnp.where(r >= c, 1.0, 0.0).astype(BF16)
    cum = _dot_sel(tril_bf, lf) + carry[...]
    cum_ref[...] = cum
    carry[...] = cum[tm - 1:tm, :]


def _logf_cumsum(z_tail, b_f_pad, *, batch, seq, tm):
    nblk = seq // tm
    return pl.pallas_call(
        _logf_kernel,
        grid=(batch, nblk),
        in_specs=[pl.BlockSpec((tm, LANE), lambda b, i: (b * nblk + i, 0)),
                  pl.BlockSpec((1, LANE), lambda b, i: (0, 0))],
        out_specs=[pl.BlockSpec((tm, LANE), lambda b, i: (b * nblk + i, 0)),
                   pl.BlockSpec((tm, LANE), lambda b, i: (b * nblk + i, 0))],
        out_shape=[jax.ShapeDtypeStruct((batch * seq, LANE), F32),
                   jax.ShapeDtypeStruct((batch * seq, LANE), F32)],
        scratch_shapes=[pltpu.VMEM((1, LANE), F32)],
        compiler_params=_params(2),
        name="logf_cumsum",
    )(z_tail, b_f_pad)


def _pair_norm(x, g, lo, dh):
    x2 = x * x
    sa = jnp.sum(jnp.where(lo, x2, 0.0), axis=-1, keepdims=True)
    sb = jnp.sum(jnp.where(lo, 0.0, x2), axis=-1, keepdims=True)
    ms = jnp.where(lo, sa, sb) * (1.0 / dh)
    return x * lax.rsqrt(ms + EPS) * g


def _fox_kernel(q_ref, k_ref, v_ref, f_ref, qg_ref, kg_ref, *rest, tq, dh, precise, n_carried):
    o_ref, knt_ref, vt_ref = rest[n_carried:n_carried + 3]
    scr = rest[n_carried + 3:]
    khi, vhi = scr[0], scr[1]
    klo, vlo = (scr[2], scr[3]) if precise else (None, None)
    i = pl.program_id(2)
    S = k_ref.shape[1]
    lo = lax.broadcasted_iota(jnp.int32, (1, LANE), 1) < dh

    def put(hi_ref, lo_ref, rows, val):
        hi = val.astype(BF16)
        hi_ref[rows, :] = hi
        if precise:
            lo_ref[rows, :] = (val - hi.astype(F32)).astype(BF16)

    @pl.when(i == 0)
    def _():
        for c0 in range(0, S, tq):
            rows = slice(c0, c0 + tq)
            kn = _pair_norm(k_ref[0, rows, :], kg_ref[...], lo, dh)
            v = v_ref[0, rows, :]
            knt_ref[0, 0, :, rows] = kn.T
            vt_ref[0, 0, :, rows] = v.T
            put(khi, klo, rows, kn)
            put(vhi, vlo, rows, v)

    q = _pair_norm(q_ref[0], qg_ref[...], lo, dh) * (dh ** -0.5)
    qs = tuple(_split2(qq) if precise else (qq.astype(BF16), None)
               for qq in (jnp.where(lo, q, 0.0), jnp.where(lo, 0.0, q)))
    rr = lax.broadcasted_iota(jnp.int32, (tq, tq), 0)
    cc = lax.broadcasted_iota(jnp.int32, (tq, tq), 1)
    causal = cc <= rr

    def chunk(j, carry, masked):
        rows = pl.ds(pl.multiple_of(j * tq, tq), tq)
        kh, vh = khi[rows, :], vhi[rows, :]
        out = []
        for hh in range(2):
            m, l, acc = carry[hh]
            fj = f_ref[0, 0, hh, pl.ds(j, 1), :]
            qh, ql = qs[hh]
            s = _dot_nt(qh, kh)
            if precise:
                s = s + (_dot_nt(qh, klo[rows, :]) + _dot_nt(ql, kh))
            s = s - fj
            if masked:
                s = jnp.where(causal, s, NEG)
            m_new = jnp.maximum(m, jnp.max(s, axis=-1, keepdims=True))
            alpha = jnp.exp(m - m_new)
            p = jnp.exp(s - m_new)
            l = alpha * l + jnp.sum(p, axis=-1, keepdims=True)
            ph = p.astype(BF16)
            pv = _dot(ph, vh)
            if precise:
                pl_ = (p - ph.astype(F32)).astype(BF16)
                pv = pv + (_dot(ph, vlo[rows, :]) + _dot(pl_, vh))
            acc = alpha * acc + pv
            out.append((m_new, l, acc))
        return tuple(out)

    init = tuple((jnp.full((tq, 1), NEG, F32), jnp.zeros((tq, 1), F32), jnp.zeros((tq, LANE), F32))
                 for _ in range(2))
    carry = lax.fori_loop(0, i, lambda j, cr: chunk(j, cr, False), init)
    (ma, la, acca), (mb, lb, accb) = chunk(i, carry, True)
    o_ref[0] = jnp.where(lo, acca / la, accb / lb).astype(o_ref.dtype)


def _fox_prompt(z3, fcum, qg, kg, carried, *, slot, n_slots, tq, precise):
    B, S, N3 = z3.shape
    dh = qg.shape[0]
    H = N3 // (3 * dh)
    HP = H // 2
    nq = S // tq
    f5 = fcum.reshape(B, HP, 2, nq, tq)
    g2 = lambda g: jnp.concatenate([g, g]).reshape(1, LANE)
    carried = [] if carried is None else list(carried)
    n_in = 6
    stack = jax.ShapeDtypeStruct((n_slots, B, H * dh, S), F32)
    return pl.pallas_call(
        functools.partial(_fox_kernel, tq=tq, dh=dh, precise=precise, n_carried=len(carried)),
        grid=(B, HP, nq),
        in_specs=[pl.BlockSpec((1, tq, LANE), lambda b, p, i: (b, i, p)),
                  pl.BlockSpec((1, S, LANE), lambda b, p, i: (b, 0, HP + p)),
                  pl.BlockSpec((1, S, LANE), lambda b, p, i: (b, 0, 2 * HP + p)),
                  pl.BlockSpec((1, 1, 2, nq, tq), lambda b, p, i: (b, p, 0, 0, 0)),
                  pl.BlockSpec((1, LANE), lambda b, p, i: (0, 0)),
                  pl.BlockSpec((1, LANE), lambda b, p, i: (0, 0))]
                 + [pl.BlockSpec(memory_space=pl.ANY)] * len(carried),
        out_specs=[pl.BlockSpec((1, tq, LANE), lambda b, p, i: (b, i, p)),
                   pl.BlockSpec((1, 1, LANE, S), lambda b, p, i: (slot, b, p, 0)),
                   pl.BlockSpec((1, 1, LANE, S), lambda b, p, i: (slot, b, p, 0))],
        out_shape=[jax.ShapeDtypeStruct((B, S, H * dh), F32 if precise else BF16), stack, stack],
        input_output_aliases={n_in + j: 1 + j for j in range(len(carried))},
        scratch_shapes=[pltpu.VMEM((S, LANE), BF16), pltpu.VMEM((S, LANE), BF16)] * (2 if precise else 1),
        compiler_params=_params(3),
        name="fox_attention_prompt",
    )(z3, z3, z3, f5, g2(qg), g2(kg), *carried)


def _head_sel(n_heads, dh):
    W = n_heads * dh
    col_head = jnp.arange(W, dtype=jnp.int32) // dh
    G = (col_head[:, None] == jnp.arange(LANE, dtype=jnp.int32)[None, :]).astype(BF16)
    return G, G.T


def _qknorm_kernel(q_ref, k_ref, f_ref, qg_ref, kg_ref, bf_ref, g_ref, e_ref, qo_ref, ko_ref, lf_ref, *, dh):
    G = g_ref[...]
    E = e_ref[...]

    def norm(x, gain):
        ms = _x_dot_sel(x * x, G) * (1.0 / dh)
        return x * _x_dot_sel(lax.rsqrt(ms + EPS), E) * gain

    qo_ref[...] = norm(q_ref[...], qg_ref[...]) * (dh ** -0.5)
    ko_ref[...] = norm(k_ref[...], kg_ref[...])
    lf_ref[...] = jax.nn.log_sigmoid(f_ref[...] + bf_ref[...])


def _qknorm_sample(z, z_tail, qg, kg, b_f_pad, G, E, *, n_heads):
    Bs = z.shape[0]
    dh = qg.shape[0]
    W = n_heads * dh
    tile = lambda g: jnp.tile(g, n_heads).reshape(1, W)
    full = lambda a: pl.BlockSpec(a.shape, lambda i: (0,) * a.ndim)
    return pl.pallas_call(
        functools.partial(_qknorm_kernel, dh=dh),
        grid=(1,),
        in_specs=[pl.BlockSpec((Bs, W), lambda i: (0, 0)),
                  pl.BlockSpec((Bs, W), lambda i: (0, 1)),
                  full(z_tail), pl.BlockSpec((1, W), lambda i: (0, 0)), pl.BlockSpec((1, W), lambda i: (0, 0)),
                  full(b_f_pad), full(G), full(E)],
        out_specs=[pl.BlockSpec((Bs, W), lambda i: (0, 0)),
                   pl.BlockSpec((Bs, W), lambda i: (0, 0)),
                   pl.BlockSpec((Bs, LANE), lambda i: (0, 0))],
        out_shape=[jax.ShapeDtypeStruct((Bs, W), F32), jax.ShapeDtypeStruct((Bs, W), F32),
                   jax.ShapeDtypeStruct((Bs, LANE), F32)],
        compiler_params=_params(1),
        name="qknorm_sample",
    )(z, z, z_tail, tile(qg), tile(kg), b_f_pad, G, E)


def _decode_kernel(pt_ref, q_ref, kn_ref, vn_ref, lfn_ref, g_ref, *rest, pages, n_heads, dh):
    H = n_heads
    k_refs = rest[:pages]
    v_refs = rest[pages:2 * pages]
    lf_refs = rest[2 * pages:3 * pages]
    o_ref = rest[3 * pages]
    qb, acc, m_scr, l_scr, c_scr, s_scr, p_scr, a_scr = rest[3 * pages + 1:]
    j = pl.program_id(1)
    P = k_refs[0].shape[4]
    lo = lax.broadcasted_iota(jnp.int32, (1, LANE), 1) < dh

    def lane_col(row):
        return jnp.broadcast_to(row, (LANE, LANE)).T

    @pl.when(j == 0)
    def _():
        q = q_ref[0]
        for hp in range(H // 2):
            t = lane_col(q[:, hp * LANE:(hp + 1) * LANE])
            qb[2 * hp] = t[0:dh]
            qb[2 * hp + 1] = t[dh:2 * dh]
        m_scr[...] = jnp.full(m_scr.shape, NEG, F32)
        l_scr[...] = jnp.zeros(l_scr.shape, F32)
        c_scr[...] = jnp.zeros(c_scr.shape, F32)
        acc[...] = jnp.zeros(acc.shape, F32)

    r = lax.broadcasted_iota(jnp.int32, (P, P), 0)
    c = lax.broadcasted_iota(jnp.int32, (P, P), 1)
    triu_bf = jnp.where(r <= c, 1.0, 0.0).astype(BF16)

    for p in range(pages):
        for h in range(H):
            s_scr[h:h + 1, :] = jnp.sum(qb[h] * k_refs[p][0, 0, h], axis=0, keepdims=True)
        fcum = _x_dot_sel(lf_refs[p][0, 0], triu_bf) + c_scr[...]
        c_scr[...] = jnp.broadcast_to(fcum[:, P - 1:P], (H, P))
        s2 = s_scr[...] - fcum
        m_old = m_scr[...]
        m_new = jnp.maximum(m_old, jnp.max(s2, axis=1, keepdims=True))
        alpha = jnp.exp(m_old - m_new)
        pr = jnp.exp(s2 - m_new)
        l_scr[...] = alpha * l_scr[...] + jnp.sum(pr, axis=1, keepdims=True)
        m_scr[...] = m_new
        p_scr[...] = pr
        a_scr[...] = alpha
        for h in range(H):
            acc[h] = acc[h] * a_scr[h:h + 1, :] + p_scr[h:h + 1, :] * v_refs[p][0, 0, h]

    @pl.when(j == pl.num_programs(1) - 1)
    def _():
        q = q_ref[0]
        vn = vn_ref[0]
        prod = jnp.broadcast_to(q * kn_ref[0], (8, q.shape[1]))
        ph, plo = _split2(prod)
        s_row = (_dot(ph, g_ref[...]) + _dot(plo, g_ref[...]))[0:1, :]
        s2 = lane_col(s_row)[0:H] - (c_scr[...] + lane_col(lfn_ref[0])[0:H])
        m_old = m_scr[...]
        m_new = jnp.maximum(m_old, s2)
        alpha = jnp.exp(m_old - m_new)
        pn = jnp.exp(s2 - m_new)
        l = alpha * l_scr[...] + pn
        for hp in range(H // 2):
            both = jnp.concatenate([acc[2 * hp], acc[2 * hp + 1]], axis=0)
            past = jnp.sum(both.T, axis=0, keepdims=True)
            pick = lambda x: jnp.where(lo, x[2 * hp:2 * hp + 1, :], x[2 * hp + 1:2 * hp + 2, :])
            o = (past * pick(alpha) + pick(pn) * vn[:, hp * LANE:(hp + 1) * LANE]) / pick(l)
            o_ref[0, :, hp * LANE:(hp + 1) * LANE] = o


def _decode_attention(page_table, q, kn, vn, lfn, cache_kt, cache_vt, cache_lft, G, *, layer, n_heads):
    Bs, W = q.shape
    npg = page_table.shape[1]
    H, dh, P = cache_kt.shape[2:]
    pp = PAGES_PER_STEP
    pt = page_table.reshape(-1)

    def page(p):
        return lambda b, j, pt_ref: (layer, pt_ref[b * npg + j * pp + p], 0, 0, 0)

    def page4(p):
        return lambda b, j, pt_ref: (layer, pt_ref[b * npg + j * pp + p], 0, 0)

    row = lambda n: pl.BlockSpec((1, 1, n), lambda b, j, pt_ref: (b, 0, 0))
    in_specs = [row(W), row(W), row(W), row(LANE),
                pl.BlockSpec(G.shape, lambda b, j, pt_ref: (0, 0))]
    in_specs += [pl.BlockSpec((1, 1, H, dh, P), page(p)) for p in range(pp)]
    in_specs += [pl.BlockSpec((1, 1, H, dh, P), page(p)) for p in range(pp)]
    in_specs += [pl.BlockSpec((1, 1, H, P), page4(p)) for p in range(pp)]
    r3 = lambda a: a.reshape(Bs, 1, a.shape[1])
    hp_tile = lambda: pltpu.VMEM((H, P), F32)
    out = pl.pallas_call(
        functools.partial(_decode_kernel, pages=pp, n_heads=n_heads, dh=dh),
        grid_spec=pltpu.PrefetchScalarGridSpec(
            num_scalar_prefetch=1,
            grid=(Bs, npg // pp),
            in_specs=in_specs,
            out_specs=pl.BlockSpec((1, 1, W), lambda b, j, pt_ref: (b, 0, 0)),
            scratch_shapes=[pltpu.VMEM((H, dh, P), F32), pltpu.VMEM((H, dh, P), F32),
                            hp_tile(), hp_tile(), hp_tile(), hp_tile(), hp_tile(), hp_tile()]),
        out_shape=jax.ShapeDtypeStruct((Bs, 1, W), F32),
        compiler_params=_params(2),
        name="fox_decode_paged",
    )(pt, r3(q), r3(kn), r3(vn), r3(lfn), G,
      *([cache_kt] * pp), *([cache_vt] * pp), *([cache_lft] * pp))
    return out.reshape(Bs, W)


def _router_kernel(x_ref, g_ref, sc_ref, sh_ref, wr_ref, br_ref, h_ref, lg_ref):
    h = _rms_mod(x_ref[...], g_ref[...], sc_ref[0], sh_ref[0])
    h_ref[...] = h
    lg_ref[...] = _dot3(h, wr_ref[...]) + br_ref[...]


def _router(x, g, sc, sh, wr_pad, br_pad, *, tm, rows_per_mod):
    M, D = x.shape
    bpm = rows_per_mod // tm
    R = sc.shape[1]
    mod = pl.BlockSpec((1, R, D), lambda i: (i // bpm, 0, 0))
    return pl.pallas_call(
        _router_kernel,
        grid=(M // tm,),
        in_specs=[pl.BlockSpec((tm, D), lambda i: (i, 0)),
                  pl.BlockSpec((1, D), lambda i: (0, 0)),
                  mod, mod,
                  pl.BlockSpec((D, LANE), lambda i: (0, 0)),
                  pl.BlockSpec((1, LANE), lambda i: (0, 0))],
        out_specs=[pl.BlockSpec((tm, D), lambda i: (i, 0)),
                   pl.BlockSpec((tm, LANE), lambda i: (i, 0))],
        out_shape=[jax.ShapeDtypeStruct((M, D), F32), jax.ShapeDtypeStruct((M, LANE), F32)],
        compiler_params=_params(1),
        name="router",
    )(x, g, sc, sh, wr_pad, br_pad)


def _moe_kernel(be_ref, na_ref, x_ref, w1_ref, b1_ref, w2_ref, b2_ref, o_ref, *scr, f_chunk, precise):
    w1hi, w2hi = scr[0], scr[1]
    w1lo, w2lo = (scr[2], scr[3]) if precise else (None, None)
    i = pl.program_id(0)
    active = i < na_ref[0]
    e = be_ref[i]
    prev = be_ref[jnp.maximum(i - 1, 0)]
    first = jnp.logical_or(i == 0, e != prev)
    FF = w2hi.shape[0]

    @pl.when(jnp.logical_and(active, first))
    def _():
        _cast_rows(lambda sl: w1_ref[0, 0, sl, :], w1hi, w1lo)
        _cast_rows(lambda sl: w2_ref[0, 0, sl, :], w2hi, w2lo)

    @pl.when(active)
    def _():
        x = x_ref[...]
        xh = x.astype(BF16)
        xl = (x - xh.astype(F32)).astype(BF16) if precise else None

        def mm1(cols):
            out = _dot(xh, w1hi[:, cols])
            if precise:
                out = out + (_dot(xh, w1lo[:, cols]) + _dot(xl, w1hi[:, cols]))
            return out

        acc = None
        for f0 in range(0, FF, f_chunk):
            glu = mm1(slice(f0, f0 + f_chunk)) + b1_ref[0, 0, :, f0:f0 + f_chunk]
            lin = mm1(slice(FF + f0, FF + f0 + f_chunk)) + b1_ref[0, 0, :, FF + f0:FF + f0 + f_chunk]
            glu = jnp.minimum(glu, SWIGLU_LIMIT)
            lin = jnp.clip(lin, -SWIGLU_LIMIT, SWIGLU_LIMIT)
            a = glu * jax.nn.sigmoid(SWIGLU_ALPHA * glu) * (lin + 1.0)
            t = _mm_w(a, w2hi[f0:f0 + f_chunk, :], w2lo[f0:f0 + f_chunk, :] if precise else None)
            acc = t if acc is None else acc + t
        o_ref[...] = acc + b2_ref[0, 0]

    @pl.when(jnp.logical_not(active))
    def _():
        o_ref[...] = jnp.zeros(o_ref.shape, F32)


def _moe_experts(blk_e, n_active, xb, w1, b1, w2, b2, *, layer, bm, precise):
    NS = xb.shape[0]
    L, E, D, F2 = w1.shape
    FF = F2 // 2
    nb = NS // bm
    return pl.pallas_call(
        functools.partial(_moe_kernel, f_chunk=512, precise=precise),
        grid_spec=pltpu.PrefetchScalarGridSpec(
            num_scalar_prefetch=2,
            grid=(nb,),
            in_specs=[pl.BlockSpec((bm, D), lambda i, be, na: (i, 0)),
                      pl.BlockSpec((1, 1, D, F2), lambda i, be, na: (layer, be[i], 0, 0)),
                      pl.BlockSpec((1, 1, 1, F2), lambda i, be, na: (layer, be[i], 0, 0)),
                      pl.BlockSpec((1, 1, FF, D), lambda i, be, na: (layer, be[i], 0, 0)),
                      pl.BlockSpec((1, 1, 1, D), lambda i, be, na: (layer, be[i], 0, 0))],
            out_specs=pl.BlockSpec((bm, D), lambda i, be, na: (i, 0)),
            scratch_shapes=[pltpu.VMEM((D, F2), BF16), pltpu.VMEM((FF, D), BF16)] * (2 if precise else 1)),
        out_shape=jax.ShapeDtypeStruct((NS, D), F32),
        compiler_params=_params(1),
        name="moe_experts",
    )(blk_e, n_active, xb, w1, b1.reshape(L, E, 1, F2), w2, b2.reshape(L, E, 1, D))


def _moe_ffn(h_all, logits, w1, b1, w2, b2, *, layer, precise):
    T = h_all.shape[0]
    D = w2.shape[3]
    bm = MOE_TILE
    top_v, top_i = lax.top_k(logits, TOP_K)
    gate = jax.nn.softmax(top_v, axis=-1)
    n = T * TOP_K
    e_km = top_i.T.reshape(-1).astype(jnp.int32)
    t_km = jnp.tile(jnp.arange(T, dtype=jnp.int32), TOP_K)
    experts = jnp.arange(N_EXPERTS, dtype=jnp.int32)
    onehot = (e_km[:, None] == experts[None, :]).astype(jnp.int32)
    csum = jnp.cumsum(onehot, axis=0)
    counts = csum[-1]
    nblk_e = (counts + bm - 1) // bm
    blk_end = jnp.cumsum(nblk_e)
    blk_start = blk_end - nblk_e
    dest = jnp.sum(onehot * (csum - 1 + (blk_start * bm)[None, :]), axis=1)
    nb = -(-(n + N_EXPERTS * (bm - 1)) // bm)
    n_active = blk_end[-1]
    bidx = jnp.minimum(jnp.arange(nb, dtype=jnp.int32), n_active - 1)
    blk_e = jnp.minimum(jnp.sum((blk_end[None, :] <= bidx[:, None]).astype(jnp.int32), axis=1), N_EXPERTS - 1)
    slot_tok = jnp.zeros((nb * bm,), jnp.int32).at[dest].set(t_km, mode="promise_in_bounds", unique_indices=True)
    xb = h_all.at[slot_tok].get(mode="promise_in_bounds")
    yb = _moe_experts(blk_e, n_active.reshape(1).astype(jnp.int32), xb, w1, b1, w2, b2, layer=layer, bm=bm,
                      precise=precise)
    ya = yb.at[dest].get(mode="promise_in_bounds").reshape(TOP_K, T, D)
    return jnp.sum(ya * gate.T[:, :, None], axis=0)


def kernel(x_prompt, x_sample, cache_k, cache_v, cache_logf, state_delta, state_conv, page_table, c_prompt, c_sample, norm_g, w_ada, b_ada, w_in_even, a_ln_g, a_ln_b, a_ws, a_bs, conv_w, A_log, dt_bias, onorm_g, w_out_even, w_in_odd, b_f, q_norm_g, k_norm_g, w_out_odd, w_router, b_router, w_mlp1, b_mlp1, w_mlp2, b_mlp2):
    Bp, Sp, D = x_prompt.shape
    Bs = x_sample.shape[0]
    depth = norm_g.shape[0]
    Mp = Bp * Sp
    a_heads = a_ws.shape[1]
    a_width = a_heads * LANE
    b_heads = A_log.shape[1]
    b_width = b_heads * LANE
    c_heads = b_f.shape[1]
    c_dh = q_norm_g.shape[1]
    c_width = c_heads * c_dh
    tm = ROW_TILE

    xp = x_prompt.reshape(Mp, D)
    xs = x_sample.reshape(Bs, D)
    mods = _ada_all(jnp.concatenate([c_prompt, c_sample], axis=0), w_ada, b_ada)
    ckt = jnp.transpose(cache_k, (0, 1, 3, 4, 2))
    cvt = jnp.transpose(cache_v, (0, 1, 3, 4, 2))
    clft = jnp.transpose(cache_logf, (0, 1, 3, 2))
    G, E = _head_sel(c_heads, c_dh)

    def pad_cols(w):
        return jnp.pad(w, ((0, 0), (0, LANE - w.shape[1])))

    fp_l, ks_l, vs_l, fs_l = [], [], [], []
    kv_stacks = None
    dp_l, cp_l, ds_l, cs_l, av_l = [], [], [], [], []

    for l in range(depth):
        i = l // 2
        pm = l < PRECISE_MIXER_LAYERS
        m6 = mods[l].reshape(Bp + Bs, 6, D)
        mp = [m6[:Bp, j].reshape(Bp, 1, D) for j in range(6)]
        ms = [m6[Bp:, j].reshape(1, Bs, D) for j in range(6)]
        g1 = norm_g[l, 0].reshape(1, D)
        g2 = norm_g[l, 1].reshape(1, D)
        if l % 2 == 0:
            n_main = 2 * a_width + 4 * b_width
            w_main = w_in_even[i][:, :n_main]
            w_tail = pad_cols(w_in_even[i][:, n_main:])
            zp, ztp = _proj(xp, g1, mp[1], mp[0], w_main, w_tail, tm=tm, rows_per_mod=Sp, precise=pm)
            zs, zts = _proj(xs, g1, ms[1], ms[0], w_main, w_tail, tm=Bs, rows_per_mod=Bs, precise=True)
            ya_p = _gmlp(zp, a_ln_g[i], a_ln_b[i], a_ws[i], a_bs[i], tm=2 * CHUNK, precise=pm)
            ob_p, sfin_p = _delta_chunked(zp, ztp, conv_w[i], A_log[i], dt_bias[i], onorm_g[i],
                                          batch=Bp, seq=Sp, tm=4 * DN_CHUNK, precise=pm)
            ya_s, vrow_s, ob_s, cnew_s, snew_s = _even_step(
                zs, zts, state_conv[i], state_delta[i], a_ln_g[i], a_ln_b[i], a_ws[i], a_bs[i],
                conv_w[i], A_log[i], dt_bias[i], onorm_g[i], nb=8)
            xp = _outproj([ya_p, ob_p], w_out_even[i], xp, mp[2], tm=tm, rows_per_mod=Sp, precise=pm)
            xs = _outproj([ya_s, ob_s], w_out_even[i], xs, ms[2], tm=Bs, rows_per_mod=Bs, precise=True)
            conv_tail = zp.reshape(Bp, Sp, n_main)[:, Sp - (CONV_W - 1):, 2 * a_width:2 * a_width + 3 * b_width]
            dp_l.append(sfin_p)
            cp_l.append(conv_tail)
            ds_l.append(snew_s)
            cs_l.append(cnew_s)
            av_l.append(vrow_s.reshape(Bs, 1, a_width))
        else:
            n_main = 3 * c_width
            w_main = w_in_odd[i][:, :n_main]
            w_tail = pad_cols(w_in_odd[i][:, n_main:])
            bf_pad = jnp.zeros((1, LANE), F32).at[0, :c_heads].set(b_f[i])
            zp, ztp = _proj(xp, g1, mp[1], mp[0], w_main, w_tail, tm=tm, rows_per_mod=Sp, precise=pm)
            zs, zts = _proj(xs, g1, ms[1], ms[0], w_main, w_tail, tm=Bs, rows_per_mod=Bs, precise=True)
            lf_p, fc_p = _logf_cumsum(ztp, bf_pad, batch=Bp, seq=Sp, tm=512)
            lf_p = lf_p[:, :c_heads].reshape(Bp, Sp, c_heads)
            fcum = jnp.swapaxes(fc_p[:, :c_heads].reshape(Bp, Sp, c_heads), 1, 2)
            o_p, knt_all, vt_all = _fox_prompt(zp.reshape(Bp, Sp, n_main), fcum, q_norm_g[i], k_norm_g[i], kv_stacks,
                                               slot=i, n_slots=depth // 2, tq=512, precise=pm)
            kv_stacks = (knt_all, vt_all)
            xp = _outproj([o_p.reshape(Mp, c_width)], w_out_odd[i], xp, mp[2], tm=tm, rows_per_mod=Sp, precise=pm)
            q_s, k_s, lf_s = _qknorm_sample(zs, zts, q_norm_g[i], k_norm_g[i], bf_pad, G, E, n_heads=c_heads)
            v_s = zs[:, 2 * c_width:]
            o_s = _decode_attention(page_table, q_s, k_s, v_s, lf_s, ckt, cvt, clft, G,
                                    layer=i, n_heads=c_heads)
            xs = _outproj([o_s], w_out_odd[i], xs, ms[2], tm=Bs, rows_per_mod=Bs, precise=True)
            fp_l.append(lf_p)
            ks_l.append(k_s.reshape(Bs, 1, c_heads, c_dh))
            vs_l.append(v_s.reshape(Bs, 1, c_heads, c_dh))
            fs_l.append(lf_s[:, :c_heads].reshape(Bs, 1, c_heads))
        wr_pad = pad_cols(w_router[l])
        br_pad = jnp.zeros((1, LANE), F32).at[0, :N_EXPERTS].set(b_router[l])
        hp, lgp = _router(xp, g2, mp[4], mp[3], wr_pad, br_pad, tm=tm, rows_per_mod=Sp)
        hs, lgs = _router(xs, g2, ms[4], ms[3], wr_pad, br_pad, tm=Bs, rows_per_mod=Bs)
        h_all = jnp.concatenate([hp, hs], axis=0)
        lg_all = jnp.concatenate([lgp, lgs], axis=0)[:, :N_EXPERTS]
        y_all = _moe_ffn(h_all, lg_all, w_mlp1, b_mlp1, w_mlp2, b_mlp2, layer=l, precise=l < PRECISE_MOE_LAYERS)
        xp = xp + (mp[5] * y_all[:Mp].reshape(Bp, Sp, D)).reshape(Mp, D)
        xs = xs + ms[5][0] * y_all[Mp:]

    def new_rows(t):
        return jnp.transpose(t.reshape(t.shape[0], Bp, c_heads, c_dh, Sp), (0, 1, 4, 2, 3))

    return (xp.reshape(Bp, Sp, D), xs.reshape(Bs, 1, D),
            new_rows(kv_stacks[0]), new_rows(kv_stacks[1]), jnp.stack(fp_l),
            jnp.stack(ks_l), jnp.stack(vs_l), jnp.stack(fs_l),
            jnp.stack(dp_l), jnp.stack(cp_l), jnp.stack(ds_l), jnp.stack(cs_l), jnp.stack(av_l))
```

```python
import functools

import jax
import jax.numpy as jnp
from jax import lax
from jax.experimental import pallas as pl
from jax.experimental.pallas import tpu as pltpu

F32 = jnp.float32
BF16 = jnp.bfloat16

EPS = 1e-6
CHUNK = 128
DN_CHUNK = 64
N_EXPERTS = 32
TOP_K = 4
SWIGLU_ALPHA = 1.702
SWIGLU_LIMIT = 7.0
CONV_W = 4
LANE = 128
NEG = -1e30

VMEM_LIMIT = 56 * 1024 * 1024
ROW_TILE = 512
MOE_TILE = 512
PAGES_PER_STEP = 8
PRECISE_MIXER_LAYERS = 2
PRECISE_MOE_LAYERS = 1


def _params(n_axes):
    return pltpu.CompilerParams(dimension_semantics=("arbitrary",) * n_axes,
                                vmem_limit_bytes=VMEM_LIMIT)


def _dot(a, b):
    return jnp.dot(a, b, preferred_element_type=F32)


def _dot_nt(a, b):
    return lax.dot_general(a, b, (((1,), (1,)), ((), ())), preferred_element_type=F32)


def _dot_tn(a, b):
    return lax.dot_general(a, b, (((0,), (0,)), ((), ())), preferred_element_type=F32)


def _split2(a):
    hi = a.astype(BF16)
    lo = (a - hi.astype(F32)).astype(BF16)
    return hi, lo


def _split3(a):
    hi = a.astype(BF16)
    r = a - hi.astype(F32)
    mid = r.astype(BF16)
    lo = (r - mid.astype(F32)).astype(BF16)
    return hi, mid, lo


def _dot3(a, b):
    ah, al = _split2(a)
    bh, bl = _split2(b)
    return _dot(ah, bh) + (_dot(ah, bl) + _dot(al, bh))


def _mm(a, b, precise, dot=_dot):
    if precise:
        ah, al = _split2(a)
        bh, bl = _split2(b)
        return dot(ah, bh) + (dot(ah, bl) + dot(al, bh))
    return dot(a.astype(BF16), b.astype(BF16))


def _mm_w(a, w_hi, w_lo):
    ah = a.astype(BF16)
    out = _dot(ah, w_hi)
    if w_lo is not None:
        al = (a - ah.astype(F32)).astype(BF16)
        out = out + (_dot(ah, w_lo) + _dot(al, w_hi))
    return out


def _dot_sel(sel_bf16, x):
    hi, mid, lo = _split3(x)
    return _dot(sel_bf16, hi) + (_dot(sel_bf16, mid) + _dot(sel_bf16, lo))


def _x_dot_sel(x, sel_bf16):
    hi, mid, lo = _split3(x)
    return _dot(hi, sel_bf16) + (_dot(mid, sel_bf16) + _dot(lo, sel_bf16))


def _rms_mod(x, g, sc, sh):
    ms = jnp.mean(x * x, axis=-1, keepdims=True)
    return (x * lax.rsqrt(ms + EPS) * g) * (1.0 + sc) + sh


def _gelu(x):
    return 0.5 * x * (1.0 + lax.erf(x * 0.7071067811865476))


def _silu(x):
    return x * jax.nn.sigmoid(x)


def _cast_rows(load, hi_ref, lo_ref, rows=128):
    n = hi_ref.shape[0] // rows

    def body(r, c):
        sl = pl.ds(pl.multiple_of(r * rows, rows), rows)
        w = load(sl)
        hi = w.astype(BF16)
        hi_ref[sl, :] = hi
        if lo_ref is not None:
            lo_ref[sl, :] = (w - hi.astype(F32)).astype(BF16)
        return c

    lax.fori_loop(0, n, body, 0)


def _ada_kernel(c_ref, w_ref, b_ref, o_ref):
    c = c_ref[...]
    o_ref[0] = _dot3(_silu(c), w_ref[0]) + b_ref[0]


def _ada_all(c_all, w_ada, b_ada):
    L, D, N = w_ada.shape
    nb = c_all.shape[0]
    tn = 1536
    return pl.pallas_call(
        _ada_kernel,
        grid=(L, N // tn),
        in_specs=[pl.BlockSpec((nb, D), lambda l, j: (0, 0)),
                  pl.BlockSpec((1, D, tn), lambda l, j: (l, 0, j)),
                  pl.BlockSpec((1, 1, tn), lambda l, j: (l, 0, j))],
        out_specs=pl.BlockSpec((1, nb, tn), lambda l, j: (l, 0, j)),
        out_shape=jax.ShapeDtypeStruct((L, nb, N), F32),
        compiler_params=_params(2),
        name="ada_mod",
    )(c_all, w_ada, b_ada.reshape(L, 1, N))


def _proj_kernel(x_ref, g_ref, sc_ref, sh_ref, w_ref, wt_ref, o_ref, ot_ref, *scr, n_chunk, precise):
    whi, wthi = scr[0], scr[1]
    wlo, wtlo = (scr[2], scr[3]) if precise else (None, None)

    @pl.when(pl.program_id(0) == 0)
    def _():
        _cast_rows(lambda sl: w_ref[sl, :], whi, wlo)
        _cast_rows(lambda sl: wt_ref[sl, :], wthi, wtlo)

    h = _rms_mod(x_ref[...], g_ref[...], sc_ref[0], sh_ref[0])
    hh = h.astype(BF16)
    hl = (h - hh.astype(F32)).astype(BF16) if precise else None

    def mm(w_hi, w_lo, cols):
        out = _dot(hh, w_hi[:, cols])
        if precise:
            out = out + (_dot(hh, w_lo[:, cols]) + _dot(hl, w_hi[:, cols]))
        return out

    for n0 in range(0, o_ref.shape[1], n_chunk):
        o_ref[:, n0:n0 + n_chunk] = mm(whi, wlo, slice(n0, n0 + n_chunk))
    ot_ref[...] = mm(wthi, wtlo, slice(0, ot_ref.shape[1]))


def _proj(x, g, sc, sh, w, w_tail, *, tm, rows_per_mod, precise):
    M, D = x.shape
    N = w.shape[1]
    NT = w_tail.shape[1]
    bpm = rows_per_mod // tm
    R = sc.shape[1]
    mod = pl.BlockSpec((1, R, D), lambda i: (i // bpm, 0, 0))
    parts = 2 if precise else 1
    return pl.pallas_call(
        functools.partial(_proj_kernel, n_chunk=512, precise=precise),
        grid=(M // tm,),
        in_specs=[pl.BlockSpec((tm, D), lambda i: (i, 0)),
                  pl.BlockSpec((1, D), lambda i: (0, 0)),
                  mod, mod,
                  pl.BlockSpec((D, N), lambda i: (0, 0), pipeline_mode=pl.Buffered(1)),
                  pl.BlockSpec((D, NT), lambda i: (0, 0), pipeline_mode=pl.Buffered(1))],
        out_specs=[pl.BlockSpec((tm, N), lambda i: (i, 0)),
                   pl.BlockSpec((tm, NT), lambda i: (i, 0))],
        out_shape=[jax.ShapeDtypeStruct((M, N), F32), jax.ShapeDtypeStruct((M, NT), F32)],
        scratch_shapes=[pltpu.VMEM((D, N), BF16), pltpu.VMEM((D, NT), BF16)] * parts,
        compiler_params=_params(1),
        name="norm_mod_proj",
    )(x, g, sc, sh, w, w_tail)


def _outproj_kernel(*refs, n_in, precise):
    ys = refs[:n_in]
    w_ref, x_ref, gate_ref, o_ref, whi = refs[n_in:n_in + 5]
    wlo = refs[n_in + 5] if precise else None

    @pl.when(pl.program_id(0) == 0)
    def _():
        _cast_rows(lambda sl: w_ref[sl, :], whi, wlo)

    acc = None
    k0 = 0
    for y_ref in ys:
        kk = y_ref.shape[1]
        t = _mm_w(y_ref[...], whi[k0:k0 + kk, :], wlo[k0:k0 + kk, :] if precise else None)
        acc = t if acc is None else acc + t
        k0 += kk
    o_ref[...] = x_ref[...] + gate_ref[0] * acc


def _outproj(ys, w, x, gate, *, tm, rows_per_mod, precise):
    M, D = x.shape
    K = w.shape[0]
    bpm = rows_per_mod // tm
    R = gate.shape[1]
    in_specs = [pl.BlockSpec((tm, y.shape[1]), lambda i: (i, 0)) for y in ys]
    in_specs += [pl.BlockSpec((K, D), lambda i: (0, 0), pipeline_mode=pl.Buffered(1)),
                 pl.BlockSpec((tm, D), lambda i: (i, 0)),
                 pl.BlockSpec((1, R, D), lambda i: (i // bpm, 0, 0))]
    return pl.pallas_call(
        functools.partial(_outproj_kernel, n_in=len(ys), precise=precise),
        grid=(M // tm,),
        in_specs=in_specs,
        out_specs=pl.BlockSpec((tm, D), lambda i: (i, 0)),
        out_shape=jax.ShapeDtypeStruct((M, D), F32),
        scratch_shapes=[pltpu.VMEM((K, D), BF16)] * (2 if precise else 1),
        compiler_params=_params(1),
        name="out_proj_residual",
    )(*ys, w, x, gate)


def _gmlp_kernel(u_ref, v_ref, lng_ref, lnb_ref, ws_ref, bst_ref, o_ref, *, n_heads, precise):
    tm = u_ref.shape[0]
    u = _gelu(u_ref[...])
    v = _gelu(v_ref[...])
    r = lax.broadcasted_iota(jnp.int32, (CHUNK, CHUNK), 0)
    c = lax.broadcasted_iota(jnp.int32, (CHUNK, CHUNK), 1)
    tril = r >= c
    for h in range(n_heads):
        ls = slice(h * LANE, (h + 1) * LANE)
        vh = v[:, ls]
        mu = jnp.mean(vh, axis=-1, keepdims=True)
        d = vh - mu
        var = jnp.mean(d * d, axis=-1, keepdims=True)
        vn = d * lax.rsqrt(var + EPS) * lng_ref[:, ls] + lnb_ref[:, ls]
        wm = jnp.where(tril, ws_ref[h], 0.0)
        bias = bst_ref[:, h:h + 1]
        for c0 in range(0, tm, CHUNK):
            mixed = _mm(wm, vn[c0:c0 + CHUNK], precise) + bias
            o_ref[c0:c0 + CHUNK, ls] = (u[c0:c0 + CHUNK, ls] * mixed).astype(o_ref.dtype)


def _gmlp(z, ln_g, ln_b, ws, bs, *, tm, precise):
    M = z.shape[0]
    H = ws.shape[0]
    W = H * LANE
    return pl.pallas_call(
        functools.partial(_gmlp_kernel, n_heads=H, precise=precise),
        grid=(M // tm,),
        in_specs=[pl.BlockSpec((tm, W), lambda i: (i, 0)),
                  pl.BlockSpec((tm, W), lambda i: (i, 1)),
                  pl.BlockSpec((1, W), lambda i: (0, 0)),
                  pl.BlockSpec((1, W), lambda i: (0, 0)),
                  pl.BlockSpec((H, CHUNK, CHUNK), lambda i: (0, 0, 0)),
                  pl.BlockSpec((CHUNK, H), lambda i: (0, 0))],
        out_specs=pl.BlockSpec((tm, W), lambda i: (i, 0)),
        out_shape=jax.ShapeDtypeStruct((M, W), F32 if precise else BF16),
        compiler_params=_params(1),
        name="gmlp_chunk",
    )(z, z, ln_g.reshape(1, W), ln_b.reshape(1, W), ws, bs.T)


def _delta_kernel(zq_ref, zk_ref, zv_ref, zg_ref, gz_ref, cw_ref, alog_ref, dtb_ref, og_ref,
                  o_ref, sfin_ref, xbuf, state, *, n_heads, precise):
    i = pl.program_id(1)
    tm = zq_ref.shape[0]
    H = n_heads
    W = H * LANE
    C = DN_CHUNK
    HC = H * C

    @pl.when(i == 0)
    def _():
        xbuf[0:8, :] = jnp.zeros((8, 3 * W), F32)
        state[...] = jnp.zeros(state.shape, F32)

    xbuf[8:8 + tm, 0:W] = zq_ref[...]
    xbuf[8:8 + tm, W:2 * W] = zk_ref[...]
    xbuf[8:8 + tm, 2 * W:3 * W] = zv_ref[...]
    conv = xbuf[8:8 + tm, :] * cw_ref[3:4, :]
    for s in range(1, CONV_W):
        conv = conv + xbuf[8 - s:8 - s + tm, :] * cw_ref[3 - s:4 - s, :]
    tail = xbuf[tm:tm + 8, :]
    xbuf[0:8, :] = tail
    qkv = _silu(conv)

    zg = zg_ref[...]
    g_all = -jnp.exp(alog_ref[...]) * jax.nn.softplus(zg + dtb_ref[...])
    beta_all = jax.nn.sigmoid(zg)

    r = lax.broadcasted_iota(jnp.int32, (C, C), 0)
    c = lax.broadcasted_iota(jnp.int32, (C, C), 1)
    tril_bf = jnp.where(r >= c, 1.0, 0.0).astype(BF16)
    rr = lax.broadcasted_iota(jnp.int32, (HC, HC), 0)
    cc = lax.broadcasted_iota(jnp.int32, (HC, HC), 1)
    same = (rr // C) == (cc // C)
    incl = jnp.logical_and(same, rr >= cc)
    strict = jnp.logical_and(same, rr > cc)
    eye = rr == cc
    eye_f = jnp.where(eye, 1.0, 0.0)

    chunks = []
    for c0 in range(0, tm, C):
        rows = slice(c0, c0 + C)
        gc_all = _dot_sel(tril_bf, g_all[rows])
        qs, ks, vs, gcs, bts, gls = [], [], [], [], [], []
        for h in range(H):
            q = qkv[rows, h * LANE:(h + 1) * LANE]
            k = qkv[rows, W + h * LANE:W + (h + 1) * LANE]
            qs.append(q * lax.rsqrt(jnp.sum(q * q, axis=-1, keepdims=True) + EPS) * (LANE ** -0.5))
            ks.append(k * lax.rsqrt(jnp.sum(k * k, axis=-1, keepdims=True) + EPS))
            vs.append(qkv[rows, 2 * W + h * LANE:2 * W + (h + 1) * LANE])
            gc = gc_all[:, h:h + 1]
            gcs.append(gc)
            gls.append(jnp.broadcast_to(gc[C - 1:C, :], (C, 1)))
            bts.append(beta_all[rows, H + h:H + h + 1])
        Q = jnp.concatenate(qs, axis=0)
        K = jnp.concatenate(ks, axis=0)
        V = jnp.concatenate(vs, axis=0)
        GC = jnp.concatenate(gcs, axis=0)
        GL = jnp.concatenate(gls, axis=0)
        BT = jnp.concatenate(bts, axis=0)
        gc_row = jnp.sum(jnp.where(eye, jnp.broadcast_to(GC, (HC, HC)), 0.0), axis=0, keepdims=True)
        decay = jnp.exp(jnp.where(incl, GC - gc_row, NEG))
        EG = jnp.exp(GC)
        KB = K * BT
        L = jnp.where(strict, _mm(KB, K, precise, _dot_nt) * decay, 0.0)
        chunks.append(dict(rows=rows, Q=Q, K=K, GC=GC, GL=GL, gls=gls, decay=decay, EG=EG, KB=KB,
                           VB=V * BT, T=eye_f - L, P=_split2(L)))

    for _ in range(5):
        for ch in chunks:
            Ph, Pl = ch["P"]
            P = _dot(Ph, Ph) + (_dot(Ph, Pl) + _dot(Pl, Ph))
            Ph, Pl = _split2(P)
            Th, Tl = _split2(ch["T"])
            ch["T"] = ch["T"] + (_dot(Th, Ph) + (_dot(Th, Pl) + _dot(Tl, Ph)))
            ch["P"] = (Ph, Pl)

    for ch in chunks:
        rows, Q, K, GC, GL, gls = ch["rows"], ch["Q"], ch["K"], ch["GC"], ch["GL"], ch["gls"]
        decay, EG, KB = ch["decay"], ch["EG"], ch["KB"]
        TV = _mm(ch["T"], jnp.concatenate([ch["VB"], KB * EG], axis=1), precise)
        value = TV[:, 0:LANE]
        kcd = TV[:, LANE:2 * LANE]
        QK = jnp.where(incl, _mm(Q, K, precise, _dot_nt) * decay, 0.0)
        QG = Q * EG
        KDEC = K * jnp.exp(GL - GC)
        vnew, ointer = [], []
        for h in range(H):
            hs = slice(h * C, (h + 1) * C)
            St = state[h]
            both = _mm(jnp.concatenate([kcd[hs], QG[hs]], axis=0), St, precise)
            vn = value[hs] - both[0:C]
            vnew.append(vn)
            ointer.append(both[C:2 * C])
            state[h] = St * jnp.exp(gls[h][0:1, :]) + _mm(KDEC[hs], vn, precise, _dot_tn)
        O = jnp.concatenate(ointer, axis=0) + _mm(QK, jnp.concatenate(vnew, axis=0), precise)
        for h in range(H):
            ls = slice(h * LANE, (h + 1) * LANE)
            o = O[h * C:(h + 1) * C]
            on = o * lax.rsqrt(jnp.mean(o * o, axis=-1, keepdims=True) + EPS) * og_ref[...]
            o_ref[rows, ls] = (on * _silu(gz_ref[rows, ls])).astype(o_ref.dtype)

    @pl.when(i == pl.num_programs(1) - 1)
    def _():
        sfin_ref[0] = state[...]


def _delta_chunked(z, z_tail, cw, a_log, dtb, og, *, batch, seq, tm, precise):
    H = a_log.shape[0]
    W = H * LANE
    nblk = seq // tm
    row = lambda b, i: b * nblk + i
    pad = lambda t: jnp.zeros((1, LANE), F32).at[0, :H].set(t)
    return pl.pallas_call(
        functools.partial(_delta_kernel, n_heads=H, precise=precise),
        grid=(batch, nblk),
        in_specs=[pl.BlockSpec((tm, W), lambda b, i: (row(b, i), 2)),
                  pl.BlockSpec((tm, W), lambda b, i: (row(b, i), 3)),
                  pl.BlockSpec((tm, W), lambda b, i: (row(b, i), 4)),
                  pl.BlockSpec((tm, LANE), lambda b, i: (row(b, i), 0)),
                  pl.BlockSpec((tm, W), lambda b, i: (row(b, i), 5)),
                  pl.BlockSpec((CONV_W, 3 * W), lambda b, i: (0, 0)),
                  pl.BlockSpec((1, LANE), lambda b, i: (0, 0)),
                  pl.BlockSpec((1, LANE), lambda b, i: (0, 0)),
                  pl.BlockSpec((1, LANE), lambda b, i: (0, 0))],
        out_specs=[pl.BlockSpec((tm, W), lambda b, i: (row(b, i), 0)),
                   pl.BlockSpec((1, H, LANE, LANE), lambda b, i: (b, 0, 0, 0))],
        out_shape=[jax.ShapeDtypeStruct((batch * seq, W), F32 if precise else BF16),
                   jax.ShapeDtypeStruct((batch, H, LANE, LANE), F32)],
        scratch_shapes=[pltpu.VMEM((tm + 8, 3 * W), F32), pltpu.VMEM((H, LANE, LANE), F32)],
        compiler_params=_params(2),
        name="delta_chunked",
    )(z, z, z, z_tail, z, cw, pad(a_log), pad(dtb), og.reshape(1, LANE))


def _even_step_kernel(zu_ref, zv_ref, zq_ref, zk_ref, zvv_ref, gz_ref, zg_ref, cb_ref, s0_ref,
                      lng_ref, lnb_ref, w00_ref, b0_ref, cw_ref, alog_ref, dtb_ref, og_ref,
                      ya_ref, vrow_ref, ob_ref, cnew_ref, snew_ref, *, n_heads):
    nb = zu_ref.shape[0]
    W = n_heads * LANE
    u = _gelu(zu_ref[...])
    vg = _gelu(zv_ref[...])
    raw = jnp.concatenate([zq_ref[...], zk_ref[...], zvv_ref[...]], axis=1)
    cb = cb_ref[...]
    conv = raw * cw_ref[3:4, :]
    for s in range(CONV_W - 1):
        conv = conv + cb[:, s * 3 * W:(s + 1) * 3 * W] * cw_ref[s:s + 1, :]
    cnew_ref[:, 0:6 * W] = cb[:, 3 * W:9 * W]
    cnew_ref[:, 6 * W:9 * W] = raw
    qkv = _silu(conv)
    zg = zg_ref[...]
    g_all = -jnp.exp(alog_ref[...]) * jax.nn.softplus(zg + dtb_ref[...])
    beta_all = jax.nn.sigmoid(zg)
    a_all = jnp.exp(g_all)
    r = lax.broadcasted_iota(jnp.int32, (LANE, LANE), 0)
    c = lax.broadcasted_iota(jnp.int32, (LANE, LANE), 1)
    eye = r == c

    def col(row):
        return jnp.sum(jnp.where(eye, jnp.broadcast_to(row, (LANE, LANE)), 0.0), axis=1, keepdims=True)

    for h in range(n_heads):
        ls = slice(h * LANE, (h + 1) * LANE)
        vh = vg[:, ls]
        mu = jnp.mean(vh, axis=-1, keepdims=True)
        d = vh - mu
        var = jnp.mean(d * d, axis=-1, keepdims=True)
        vn = d * lax.rsqrt(var + EPS) * lng_ref[:, ls] + lnb_ref[:, ls]
        vrow_ref[:, ls] = vn
        ya_ref[:, ls] = u[:, ls] * (vn * w00_ref[:, ls] + b0_ref[:, ls])
        q = qkv[:, h * LANE:(h + 1) * LANE]
        k = qkv[:, W + h * LANE:W + (h + 1) * LANE]
        v = qkv[:, 2 * W + h * LANE:2 * W + (h + 1) * LANE]
        q = q * lax.rsqrt(jnp.sum(q * q, axis=-1, keepdims=True) + EPS) * (LANE ** -0.5)
        k = k * lax.rsqrt(jnp.sum(k * k, axis=-1, keepdims=True) + EPS)
        for j in range(nb):
            kc = col(k[j:j + 1])
            qc = col(q[j:j + 1])
            Sd = s0_ref[j, h] * a_all[j:j + 1, h:h + 1]
            kS = jnp.sum(kc * Sd, axis=0, keepdims=True)
            delta = (v[j:j + 1] - kS) * beta_all[j:j + 1, n_heads + h:n_heads + h + 1]
            Sn = Sd + kc * delta
            snew_ref[j, h] = Sn
            o = jnp.sum(qc * Sn, axis=0, keepdims=True)
            on = o * lax.rsqrt(jnp.mean(o * o, axis=-1, keepdims=True) + EPS) * og_ref[...]
            ob_ref[j:j + 1, ls] = on * _silu(gz_ref[j:j + 1, ls])


def _even_step(z, z_tail, conv_buf, s0, ln_g, ln_b, ws, bs, cw, a_log, dtb, og, *, nb):
    Bs = z.shape[0]
    H = a_log.shape[0]
    W = H * LANE
    pad = lambda t: jnp.zeros((1, LANE), F32).at[0, :H].set(t)
    w00 = jnp.repeat(ws[:, 0, 0], LANE).reshape(1, W)
    b0 = jnp.repeat(bs[:, 0], LANE).reshape(1, W)
    cb = conv_buf.reshape(Bs, (CONV_W - 1) * 3 * W)
    zspec = lambda j: pl.BlockSpec((nb, W), lambda i: (i, j))
    vec = lambda n: pl.BlockSpec((1, n), lambda i: (0, 0))
    ya, vrow, ob, cnew, snew = pl.pallas_call(
        functools.partial(_even_step_kernel, n_heads=H),
        grid=(Bs // nb,),
        in_specs=[zspec(0), zspec(1), zspec(2), zspec(3), zspec(4), zspec(5),
                  pl.BlockSpec((nb, LANE), lambda i: (i, 0)),
                  pl.BlockSpec((nb, 9 * W), lambda i: (i, 0)),
                  pl.BlockSpec((nb, H, LANE, LANE), lambda i: (i, 0, 0, 0)),
                  vec(W), vec(W), vec(W), vec(W),
                  pl.BlockSpec((CONV_W, 3 * W), lambda i: (0, 0)),
                  vec(LANE), vec(LANE), vec(LANE)],
        out_specs=[pl.BlockSpec((nb, W), lambda i: (i, 0)),
                   pl.BlockSpec((nb, W), lambda i: (i, 0)),
                   pl.BlockSpec((nb, W), lambda i: (i, 0)),
                   pl.BlockSpec((nb, 9 * W), lambda i: (i, 0)),
                   pl.BlockSpec((nb, H, LANE, LANE), lambda i: (i, 0, 0, 0))],
        out_shape=[jax.ShapeDtypeStruct((Bs, W), F32),
                   jax.ShapeDtypeStruct((Bs, W), F32),
                   jax.ShapeDtypeStruct((Bs, W), F32),
                   jax.ShapeDtypeStruct((Bs, 9 * W), F32),
                   jax.ShapeDtypeStruct((Bs, H, LANE, LANE), F32)],
        compiler_params=_params(1),
        name="even_step_sample",
    )(z, z, z, z, z, z, z_tail, cb, s0, ln_g.reshape(1, W), ln_b.reshape(1, W), w00, b0, cw,
      pad(a_log), pad(dtb), og.reshape(1, LANE))
    return ya, vrow, ob, cnew.reshape(Bs, CONV_W - 1, 3 * W), snew


def _logf_kernel(f_ref, bf_ref, lf_ref, cum_ref, carry):
    @pl.when(pl.program_id(1) == 0)
    def _():
        carry[...] = jnp.zeros(carry.shape, F32)

    tm = f_ref.shape[0]
    lf = jax.nn.log_sigmoid(f_ref[...] + bf_ref[...])
    lf_ref[...] = lf
    r = lax.broadcasted_iota(jnp.int32, (tm, tm), 0)
    c = lax.broadcasted_iota(jnp.int32, (tm, tm), 1)
    tril_bf = jnp.where(r >= c, 1.0, 0.0).astype(BF16)
    cum = _dot_sel(tril_bf, lf) + carry[...]
    cum_ref[...] = cum
    carry[...] = cum[tm - 1:tm, :]


def _logf_cumsum(z_tail, b_f_pad, *, batch, seq, tm):
    nblk = seq // tm
    return pl.pallas_call(
        _logf_kernel,
        grid=(batch, nblk),
        in_specs=[pl.BlockSpec((tm, LANE), lambda b, i: (b * nblk + i, 0)),
                  pl.BlockSpec((1, LANE), lambda b, i: (0, 0))],
        out_specs=[pl.BlockSpec((tm, LANE), lambda b, i: (b * nblk + i, 0)),
                   pl.BlockSpec((tm, LANE), lambda b, i: (b * nblk + i, 0))],
        out_shape=[jax.ShapeDtypeStruct((batch * seq, LANE), F32),
                   jax.ShapeDtypeStruct((batch * seq, LANE), F32)],
        scratch_shapes=[pltpu.VMEM((1, LANE), F32)],
        compiler_params=_params(2),
        name="logf_cumsum",
    )(z_tail, b_f_pad)


def _pair_norm(x, g, lo, dh):
    x2 = x * x
    sa = jnp.sum(jnp.where(lo, x2, 0.0), axis=-1, keepdims=True)
    sb = jnp.sum(jnp.where(lo, 0.0, x2), axis=-1, keepdims=True)
    ms = jnp.where(lo, sa, sb) * (1.0 / dh)
    return x * lax.rsqrt(ms + EPS) * g


def _fox_kernel(q_ref, k_ref, v_ref, f_ref, qg_ref, kg_ref, *rest, tq, dh, precise, n_carried):
    o_ref, knt_ref, vt_ref = rest[n_carried:n_carried + 3]
    scr = rest[n_carried + 3:]
    khi, vhi = scr[0], scr[1]
    klo, vlo = (scr[2], scr[3]) if precise else (None, None)
    i = pl.program_id(2)
    S = k_ref.shape[1]
    lo = lax.broadcasted_iota(jnp.int32, (1, LANE), 1) < dh

    def put(hi_ref, lo_ref, rows, val):
        hi = val.astype(BF16)
        hi_ref[rows, :] = hi
        if precise:
            lo_ref[rows, :] = (val - hi.astype(F32)).astype(BF16)

    @pl.when(i == 0)
    def _():
        for c0 in range(0, S, tq):
            rows = slice(c0, c0 + tq)
            kn = _pair_norm(k_ref[0, rows, :], kg_ref[...], lo, dh)
            v = v_ref[0, rows, :]
            knt_ref[0, 0, :, rows] = kn.T
            vt_ref[0, 0, :, rows] = v.T
            put(khi, klo, rows, kn)
            put(vhi, vlo, rows, v)

    q = _pair_norm(q_ref[0], qg_ref[...], lo, dh) * (dh ** -0.5)
    qs = tuple(_split2(qq) if precise else (qq.astype(BF16), None)
               for qq in (jnp.where(lo, q, 0.0), jnp.where(lo, 0.0, q)))
    rr = lax.broadcasted_iota(jnp.int32, (tq, tq), 0)
    cc = lax.broadcasted_iota(jnp.int32, (tq, tq), 1)
    causal = cc <= rr

    def chunk(j, carry, masked):
        rows = pl.ds(pl.multiple_of(j * tq, tq), tq)
        kh, vh = khi[rows, :], vhi[rows, :]
        out = []
        for hh in range(2):
            m, l, acc = carry[hh]
            fj = f_ref[0, 0, hh, pl.ds(j, 1), :]
            qh, ql = qs[hh]
            s = _dot_nt(qh, kh)
            if precise:
                s = s + (_dot_nt(qh, klo[rows, :]) + _dot_nt(ql, kh))
            s = s - fj
            if masked:
                s = jnp.where(causal, s, NEG)
            m_new = jnp.maximum(m, jnp.max(s, axis=-1, keepdims=True))
            alpha = jnp.exp(m - m_new)
            p = jnp.exp(s - m_new)
            l = alpha * l + jnp.sum(p, axis=-1, keepdims=True)
            ph = p.astype(BF16)
            pv = _dot(ph, vh)
            if precise:
                pl_ = (p - ph.astype(F32)).astype(BF16)
                pv = pv + (_dot(ph, vlo[rows, :]) + _dot(pl_, vh))
            acc = alpha * acc + pv
            out.append((m_new, l, acc))
        return tuple(out)

    init = tuple((jnp.full((tq, 1), NEG, F32), jnp.zeros((tq, 1), F32), jnp.zeros((tq, LANE), F32))
                 for _ in range(2))
    carry = lax.fori_loop(0, i, lambda j, cr: chunk(j, cr, False), init)
    (ma, la, acca), (mb, lb, accb) = chunk(i, carry, True)
    o_ref[0] = jnp.where(lo, acca / la, accb / lb).astype(o_ref.dtype)


def _fox_prompt(z3, fcum, qg, kg, carried, *, slot, n_slots, tq, precise):
    B, S, N3 = z3.shape
    dh = qg.shape[0]
    H = N3 // (3 * dh)
    HP = H // 2
    nq = S // tq
    f5 = fcum.reshape(B, HP, 2, nq, tq)
    g2 = lambda g: jnp.concatenate([g, g]).reshape(1, LANE)
    carried = [] if carried is None else list(carried)
    n_in = 6
    stack = jax.ShapeDtypeStruct((n_slots, B, H * dh, S), F32)
    return pl.pallas_call(
        functools.partial(_fox_kernel, tq=tq, dh=dh, precise=precise, n_carried=len(carried)),
        grid=(B, HP, nq),
        in_specs=[pl.BlockSpec((1, tq, LANE), lambda b, p, i: (b, i, p)),
                  pl.BlockSpec((1, S, LANE), lambda b, p, i: (b, 0, HP + p)),
                  pl.BlockSpec((1, S, LANE), lambda b, p, i: (b, 0, 2 * HP + p)),
                  pl.BlockSpec((1, 1, 2, nq, tq), lambda b, p, i: (b, p, 0, 0, 0)),
                  pl.BlockSpec((1, LANE), lambda b, p, i: (0, 0)),
                  pl.BlockSpec((1, LANE), lambda b, p, i: (0, 0))]
                 + [pl.BlockSpec(memory_space=pl.ANY)] * len(carried),
        out_specs=[pl.BlockSpec((1, tq, LANE), lambda b, p, i: (b, i, p)),
                   pl.BlockSpec((1, 1, LANE, S), lambda b, p, i: (slot, b, p, 0)),
                   pl.BlockSpec((1, 1, LANE, S), lambda b, p, i: (slot, b, p, 0))],
        out_shape=[jax.ShapeDtypeStruct((B, S, H * dh), F32 if precise else BF16), stack, stack],
        input_output_aliases={n_in + j: 1 + j for j in range(len(carried))},
        scratch_shapes=[pltpu.VMEM((S, LANE), BF16), pltpu.VMEM((S, LANE), BF16)] * (2 if precise else 1),
        compiler_params=_params(3),
        name="fox_attention_prompt",
    )(z3, z3, z3, f5, g2(qg), g2(kg), *carried)


def _head_sel(n_heads, dh):
    W = n_heads * dh
    col_head = jnp.arange(W, dtype=jnp.int32) // dh
    G = (col_head[:, None] == jnp.arange(LANE, dtype=jnp.int32)[None, :]).astype(BF16)
    return G, G.T


def _qknorm_kernel(q_ref, k_ref, f_ref, qg_ref, kg_ref, bf_ref, g_ref, e_ref, qo_ref, ko_ref, lf_ref, *, dh):
    G = g_ref[...]
    E = e_ref[...]

    def norm(x, gain):
        ms = _x_dot_sel(x * x, G) * (1.0 / dh)
        return x * _x_dot_sel(lax.rsqrt(ms + EPS), E) * gain

    qo_ref[...] = norm(q_ref[...], qg_ref[...]) * (dh ** -0.5)
    ko_ref[...] = norm(k_ref[...], kg_ref[...])
    lf_ref[...] = jax.nn.log_sigmoid(f_ref[...] + bf_ref[...])


def _qknorm_sample(z, z_tail, qg, kg, b_f_pad, G, E, *, n_heads):
    Bs = z.shape[0]
    dh = qg.shape[0]
    W = n_heads * dh
    tile = lambda g: jnp.tile(g, n_heads).reshape(1, W)
    full = lambda a: pl.BlockSpec(a.shape, lambda i: (0,) * a.ndim)
    return pl.pallas_call(
        functools.partial(_qknorm_kernel, dh=dh),
        grid=(1,),
        in_specs=[pl.BlockSpec((Bs, W), lambda i: (0, 0)),
                  pl.BlockSpec((Bs, W), lambda i: (0, 1)),
                  full(z_tail), pl.BlockSpec((1, W), lambda i: (0, 0)), pl.BlockSpec((1, W), lambda i: (0, 0)),
                  full(b_f_pad), full(G), full(E)],
        out_specs=[pl.BlockSpec((Bs, W), lambda i: (0, 0)),
                   pl.BlockSpec((Bs, W), lambda i: (0, 0)),
                   pl.BlockSpec((Bs, LANE), lambda i: (0, 0))],
        out_shape=[jax.ShapeDtypeStruct((Bs, W), F32), jax.ShapeDtypeStruct((Bs, W), F32),
                   jax.ShapeDtypeStruct((Bs, LANE), F32)],
        compiler_params=_params(1),
        name="qknorm_sample",
    )(z, z, z_tail, tile(qg), tile(kg), b_f_pad, G, E)


def _decode_kernel(pt_ref, q_ref, kn_ref, vn_ref, lfn_ref, g_ref, *rest, pages, n_heads, dh):
    H = n_heads
    k_refs = rest[:pages]
    v_refs = rest[pages:2 * pages]
    lf_refs = rest[2 * pages:3 * pages]
    o_ref = rest[3 * pages]
    qb, acc, m_scr, l_scr, c_scr, s_scr, p_scr, a_scr = rest[3 * pages + 1:]
    j = pl.program_id(1)
    P = k_refs[0].shape[4]
    lo = lax.broadcasted_iota(jnp.int32, (1, LANE), 1) < dh

    def lane_col(row):
        return jnp.broadcast_to(row, (LANE, LANE)).T

    @pl.when(j == 0)
    def _():
        q = q_ref[0]
        for hp in range(H // 2):
            t = lane_col(q[:, hp * LANE:(hp + 1) * LANE])
            qb[2 * hp] = t[0:dh]
            qb[2 * hp + 1] = t[dh:2 * dh]
        m_scr[...] = jnp.full(m_scr.shape, NEG, F32)
        l_scr[...] = jnp.zeros(l_scr.shape, F32)
        c_scr[...] = jnp.zeros(c_scr.shape, F32)
        acc[...] = jnp.zeros(acc.shape, F32)

    r = lax.broadcasted_iota(jnp.int32, (P, P), 0)
    c = lax.broadcasted_iota(jnp.int32, (P, P), 1)
    triu_bf = jnp.where(r <= c, 1.0, 0.0).astype(BF16)

    for p in range(pages):
        for h in range(H):
            s_scr[h:h + 1, :] = jnp.sum(qb[h] * k_refs[p][0, 0, h], axis=0, keepdims=True)
        fcum = _x_dot_sel(lf_refs[p][0, 0], triu_bf) + c_scr[...]
        c_scr[...] = jnp.broadcast_to(fcum[:, P - 1:P], (H, P))
        s2 = s_scr[...] - fcum
        m_old = m_scr[...]
        m_new = jnp.maximum(m_old, jnp.max(s2, axis=1, keepdims=True))
        alpha = jnp.exp(m_old - m_new)
        pr = jnp.exp(s2 - m_new)
        l_scr[...] = alpha * l_scr[...] + jnp.sum(pr, axis=1, keepdims=True)
        m_scr[...] = m_new
        p_scr[...] = pr
        a_scr[...] = alpha
        for h in range(H):
            acc[h] = acc[h] * a_scr[h:h + 1, :] + p_scr[h:h + 1, :] * v_refs[p][0, 0, h]

    @pl.when(j == pl.num_programs(1) - 1)
    def _():
        q = q_ref[0]
        vn = vn_ref[0]
        prod = jnp.broadcast_to(q * kn_ref[0], (8, q.shape[1]))
        ph, plo = _split2(prod)
        s_row = (_dot(ph, g_ref[...]) + _dot(plo, g_ref[...]))[0:1, :]
        s2 = lane_col(s_row)[0:H] - (c_scr[...] + lane_col(lfn_ref[0])[0:H])
        m_old = m_scr[...]
        m_new = jnp.maximum(m_old, s2)
        alpha = jnp.exp(m_old - m_new)
        pn = jnp.exp(s2 - m_new)
        l = alpha * l_scr[...] + pn
        for hp in range(H // 2):
            both = jnp.concatenate([acc[2 * hp], acc[2 * hp + 1]], axis=0)
            past = jnp.sum(both.T, axis=0, keepdims=True)
            pick = lambda x: jnp.where(lo, x[2 * hp:2 * hp + 1, :], x[2 * hp + 1:2 * hp + 2, :])
            o = (past * pick(alpha) + pick(pn) * vn[:, hp * LANE:(hp + 1) * LANE]) / pick(l)
            o_ref[0, :, hp * LANE:(hp + 1) * LANE] = o


def _decode_attention(page_table, q, kn, vn, lfn, cache_kt, cache_vt, cache_lft, G, *, layer, n_heads):
    Bs, W = q.shape
    npg = page_table.shape[1]
    H, dh, P = cache_kt.shape[2:]
    pp = PAGES_PER_STEP
    pt = page_table.reshape(-1)

    def page(p):
        return lambda b, j, pt_ref: (layer, pt_ref[b * npg + j * pp + p], 0, 0, 0)

    def page4(p):
        return lambda b, j, pt_ref: (layer, pt_ref[b * npg + j * pp + p], 0, 0)

    row = lambda n: pl.BlockSpec((1, 1, n), lambda b, j, pt_ref: (b, 0, 0))
    in_specs = [row(W), row(W), row(W), row(LANE),
                pl.BlockSpec(G.shape, lambda b, j, pt_ref: (0, 0))]
    in_specs += [pl.BlockSpec((1, 1, H, dh, P), page(p)) for p in range(pp)]
    in_specs += [pl.BlockSpec((1, 1, H, dh, P), page(p)) for p in range(pp)]
    in_specs += [pl.BlockSpec((1, 1, H, P), page4(p)) for p in range(pp)]
    r3 = lambda a: a.reshape(Bs, 1, a.shape[1])
    hp_tile = lambda: pltpu.VMEM((H, P), F32)
    out = pl.pallas_call(
        functools.partial(_decode_kernel, pages=pp, n_heads=n_heads, dh=dh),
        grid_spec=pltpu.PrefetchScalarGridSpec(
            num_scalar_prefetch=1,
            grid=(Bs, npg // pp),
            in_specs=in_specs,
            out_specs=pl.BlockSpec((1, 1, W), lambda b, j, pt_ref: (b, 0, 0)),
            scratch_shapes=[pltpu.VMEM((H, dh, P), F32), pltpu.VMEM((H, dh, P), F32),
                            hp_tile(), hp_tile(), hp_tile(), hp_tile(), hp_tile(), hp_tile()]),
        out_shape=jax.ShapeDtypeStruct((Bs, 1, W), F32),
        compiler_params=_params(2),
        name="fox_decode_paged",
    )(pt, r3(q), r3(kn), r3(vn), r3(lfn), G,
      *([cache_kt] * pp), *([cache_vt] * pp), *([cache_lft] * pp))
    return out.reshape(Bs, W)


def _router_kernel(x_ref, g_ref, sc_ref, sh_ref, wr_ref, br_ref, h_ref, lg_ref):
    h = _rms_mod(x_ref[...], g_ref[...], sc_ref[0], sh_ref[0])
    h_ref[...] = h
    lg_ref[...] = _dot3(h, wr_ref[...]) + br_ref[...]


def _router(x, g, sc, sh, wr_pad, br_pad, *, tm, rows_per_mod):
    M, D = x.shape
    bpm = rows_per_mod // tm
    R = sc.shape[1]
    mod = pl.BlockSpec((1, R, D), lambda i: (i // bpm, 0, 0))
    return pl.pallas_call(
        _router_kernel,
        grid=(M // tm,),
        in_specs=[pl.BlockSpec((tm, D), lambda i: (i, 0)),
                  pl.BlockSpec((1, D), lambda i: (0, 0)),
                  mod, mod,
                  pl.BlockSpec((D, LANE), lambda i: (0, 0)),
                  pl.BlockSpec((1, LANE), lambda i: (0, 0))],
        out_specs=[pl.BlockSpec((tm, D), lambda i: (i, 0)),
                   pl.BlockSpec((tm, LANE), lambda i: (i, 0))],
        out_shape=[jax.ShapeDtypeStruct((M, D), F32), jax.ShapeDtypeStruct((M, LANE), F32)],
        compiler_params=_params(1),
        name="router",
    )(x, g, sc, sh, wr_pad, br_pad)


def _moe_kernel(ie_ref, ib_ref, lo_ref, hi_ref, na_ref, x_ref, w1_ref, b1_ref, w2_ref, b2_ref, o_ref, *scr,
                f_chunk, precise):
    w1hi, w2hi = scr[0], scr[1]
    w1lo, w2lo = (scr[2], scr[3]) if precise else (None, None)
    i = pl.program_id(0)
    active = i < na_ref[0]
    prev = jnp.maximum(i - 1, 0)
    new_expert = jnp.logical_or(i == 0, ie_ref[i] != ie_ref[prev])
    new_block = jnp.logical_or(i == 0, ib_ref[i] != ib_ref[prev])
    FF = w2hi.shape[0]

    @pl.when(jnp.logical_and(active, new_expert))
    def _():
        _cast_rows(lambda sl: w1_ref[0, 0, sl, :], w1hi, w1lo)
        _cast_rows(lambda sl: w2_ref[0, 0, sl, :], w2hi, w2lo)

    @pl.when(active)
    def _():
        x = x_ref[...]
        xh = x.astype(BF16)
        xl = (x - xh.astype(F32)).astype(BF16) if precise else None

        def mm1(cols):
            out = _dot(xh, w1hi[:, cols])
            if precise:
                out = out + (_dot(xh, w1lo[:, cols]) + _dot(xl, w1hi[:, cols]))
            return out

        acc = None
        for f0 in range(0, FF, f_chunk):
            glu = mm1(slice(f0, f0 + f_chunk)) + b1_ref[0, 0, :, f0:f0 + f_chunk]
            lin = mm1(slice(FF + f0, FF + f0 + f_chunk)) + b1_ref[0, 0, :, FF + f0:FF + f0 + f_chunk]
            glu = jnp.minimum(glu, SWIGLU_LIMIT)
            lin = jnp.clip(lin, -SWIGLU_LIMIT, SWIGLU_LIMIT)
            a = glu * jax.nn.sigmoid(SWIGLU_ALPHA * glu) * (lin + 1.0)
            t = _mm_w(a, w2hi[f0:f0 + f_chunk, :], w2lo[f0:f0 + f_chunk, :] if precise else None)
            acc = t if acc is None else acc + t
        res = acc + b2_ref[0, 0]
        row = lax.broadcasted_iota(jnp.int32, (res.shape[0], 1), 0)
        mine = jnp.logical_and(row >= lo_ref[i], row < hi_ref[i])

        @pl.when(new_block)
        def _():
            o_ref[...] = jnp.where(mine, res, 0.0)

        @pl.when(jnp.logical_not(new_block))
        def _():
            o_ref[...] = jnp.where(mine, res, o_ref[...])


def _moe_experts(items, xb, w1, b1, w2, b2, *, layer, bm, precise):
    it_e, it_b, it_lo, it_hi, n_items = items
    NS = xb.shape[0]
    L, E, D, F2 = w1.shape
    FF = F2 // 2
    wmap = lambda i, ie, ib, lo, hi, na: (layer, ie[i], 0, 0)
    xmap = lambda i, ie, ib, lo, hi, na: (ib[i], 0)
    return pl.pallas_call(
        functools.partial(_moe_kernel, f_chunk=512, precise=precise),
        grid_spec=pltpu.PrefetchScalarGridSpec(
            num_scalar_prefetch=5,
            grid=(it_e.shape[0],),
            in_specs=[pl.BlockSpec((bm, D), xmap),
                      pl.BlockSpec((1, 1, D, F2), wmap),
                      pl.BlockSpec((1, 1, 1, F2), wmap),
                      pl.BlockSpec((1, 1, FF, D), wmap),
                      pl.BlockSpec((1, 1, 1, D), wmap)],
            out_specs=pl.BlockSpec((bm, D), xmap),
            scratch_shapes=[pltpu.VMEM((D, F2), BF16), pltpu.VMEM((FF, D), BF16)] * (2 if precise else 1)),
        out_shape=jax.ShapeDtypeStruct((NS, D), F32),
        compiler_params=_params(1),
        name="moe_experts",
    )(it_e, it_b, it_lo, it_hi, n_items, xb, w1, b1.reshape(L, E, 1, F2), w2, b2.reshape(L, E, 1, D))


def _moe_ffn(h_all, logits, w1, b1, w2, b2, *, layer, precise):
    T = h_all.shape[0]
    D = w2.shape[3]
    bm = MOE_TILE
    top_v, top_i = lax.top_k(logits, TOP_K)
    gate = jax.nn.softmax(top_v, axis=-1)
    n = T * TOP_K
    e_km = top_i.T.reshape(-1).astype(jnp.int32)
    experts = jnp.arange(N_EXPERTS, dtype=jnp.int32)
    onehot = (e_km[:, None] == experts[None, :]).astype(jnp.int32)
    csum = jnp.cumsum(onehot, axis=0)
    counts = csum[-1]
    start = jnp.cumsum(counts) - counts
    pos = jnp.sum(onehot * (csum - 1 + start[None, :]), axis=1)
    order = jnp.argsort(e_km, stable=True).astype(jnp.int32)
    nbk = -(-n // bm)
    tok = jnp.concatenate([order % T, jnp.zeros((nbk * bm - n,), jnp.int32)])
    first_blk = start // bm
    n_it = jnp.where(counts > 0, (start + counts - 1) // bm - first_blk + 1, 0)
    it_end = jnp.cumsum(n_it)
    it_start = it_end - n_it
    n_items = it_end[-1]
    idx = jnp.minimum(jnp.arange(nbk + N_EXPERTS, dtype=jnp.int32), n_items - 1)
    it_e = jnp.minimum(jnp.sum((it_end[None, :] <= idx[:, None]).astype(jnp.int32), axis=1), N_EXPERTS - 1)
    sel = (it_e[:, None] == experts[None, :]).astype(jnp.int32)
    pick = lambda v: jnp.sum(sel * v[None, :], axis=1)
    it_b = pick(first_blk) + idx - pick(it_start)
    it_lo = jnp.maximum(pick(start) - it_b * bm, 0)
    it_hi = jnp.minimum(pick(start + counts) - it_b * bm, bm)
    xb = h_all.at[tok].get(mode="promise_in_bounds")
    yb = _moe_experts((it_e, it_b, it_lo, it_hi, n_items.reshape(1).astype(jnp.int32)), xb, w1, b1, w2, b2,
                      layer=layer, bm=bm, precise=precise)
    ya = yb.at[pos].get(mode="promise_in_bounds").reshape(TOP_K, T, D)
    return ya, gate.T


def kernel(x_prompt, x_sample, cache_k, cache_v, cache_logf, state_delta, state_conv, page_table, c_prompt, c_sample, norm_g, w_ada, b_ada, w_in_even, a_ln_g, a_ln_b, a_ws, a_bs, conv_w, A_log, dt_bias, onorm_g, w_out_even, w_in_odd, b_f, q_norm_g, k_norm_g, w_out_odd, w_router, b_router, w_mlp1, b_mlp1, w_mlp2, b_mlp2):
    Bp, Sp, D = x_prompt.shape
    Bs = x_sample.shape[0]
    depth = norm_g.shape[0]
    Mp = Bp * Sp
    a_heads = a_ws.shape[1]
    a_width = a_heads * LANE
    b_heads = A_log.shape[1]
    b_width = b_heads * LANE
    c_heads = b_f.shape[1]
    c_dh = q_norm_g.shape[1]
    c_width = c_heads * c_dh
    tm = ROW_TILE

    xp = x_prompt.reshape(Mp, D)
    xs = x_sample.reshape(Bs, D)
    mods = _ada_all(jnp.concatenate([c_prompt, c_sample], axis=0), w_ada, b_ada)
    ckt = jnp.transpose(cache_k, (0, 1, 3, 4, 2))
    cvt = jnp.transpose(cache_v, (0, 1, 3, 4, 2))
    clft = jnp.transpose(cache_logf, (0, 1, 3, 2))
    G, E = _head_sel(c_heads, c_dh)

    def pad_cols(w):
        return jnp.pad(w, ((0, 0), (0, LANE - w.shape[1])))

    fp_l, ks_l, vs_l, fs_l = [], [], [], []
    kv_stacks = None
    dp_l, cp_l, ds_l, cs_l, av_l = [], [], [], [], []

    for l in range(depth):
        i = l // 2
        pm = l < PRECISE_MIXER_LAYERS
        m6 = mods[l].reshape(Bp + Bs, 6, D)
        mp = [m6[:Bp, j].reshape(Bp, 1, D) for j in range(6)]
        ms = [m6[Bp:, j].reshape(1, Bs, D) for j in range(6)]
        g1 = norm_g[l, 0].reshape(1, D)
        g2 = norm_g[l, 1].reshape(1, D)
        if l % 2 == 0:
            n_main = 2 * a_width + 4 * b_width
            w_main = w_in_even[i][:, :n_main]
            w_tail = pad_cols(w_in_even[i][:, n_main:])
            zp, ztp = _proj(xp, g1, mp[1], mp[0], w_main, w_tail, tm=tm, rows_per_mod=Sp, precise=pm)
            zs, zts = _proj(xs, g1, ms[1], ms[0], w_main, w_tail, tm=Bs, rows_per_mod=Bs, precise=True)
            ya_p = _gmlp(zp, a_ln_g[i], a_ln_b[i], a_ws[i], a_bs[i], tm=2 * CHUNK, precise=pm)
            ob_p, sfin_p = _delta_chunked(zp, ztp, conv_w[i], A_log[i], dt_bias[i], onorm_g[i],
                                          batch=Bp, seq=Sp, tm=4 * DN_CHUNK, precise=pm)
            ya_s, vrow_s, ob_s, cnew_s, snew_s = _even_step(
                zs, zts, state_conv[i], state_delta[i], a_ln_g[i], a_ln_b[i], a_ws[i], a_bs[i],
                conv_w[i], A_log[i], dt_bias[i], onorm_g[i], nb=8)
            xp = _outproj([ya_p, ob_p], w_out_even[i], xp, mp[2], tm=tm, rows_per_mod=Sp, precise=pm)
            xs = _outproj([ya_s, ob_s], w_out_even[i], xs, ms[2], tm=Bs, rows_per_mod=Bs, precise=True)
            conv_tail = zp.reshape(Bp, Sp, n_main)[:, Sp - (CONV_W - 1):, 2 * a_width:2 * a_width + 3 * b_width]
            dp_l.append(sfin_p)
            cp_l.append(conv_tail)
            ds_l.append(snew_s)
            cs_l.append(cnew_s)
            av_l.append(vrow_s.reshape(Bs, 1, a_width))
        else:
            n_main = 3 * c_width
            w_main = w_in_odd[i][:, :n_main]
            w_tail = pad_cols(w_in_odd[i][:, n_main:])
            bf_pad = jnp.zeros((1, LANE), F32).at[0, :c_heads].set(b_f[i])
            zp, ztp = _proj(xp, g1, mp[1], mp[0], w_main, w_tail, tm=tm, rows_per_mod=Sp, precise=pm)
            zs, zts = _proj(xs, g1, ms[1], ms[0], w_main, w_tail, tm=Bs, rows_per_mod=Bs, precise=True)
            lf_p, fc_p = _logf_cumsum(ztp, bf_pad, batch=Bp, seq=Sp, tm=512)
            lf_p = lf_p[:, :c_heads].reshape(Bp, Sp, c_heads)
            fcum = jnp.swapaxes(fc_p[:, :c_heads].reshape(Bp, Sp, c_heads), 1, 2)
            o_p, knt_all, vt_all = _fox_prompt(zp.reshape(Bp, Sp, n_main), fcum, q_norm_g[i], k_norm_g[i], kv_stacks,
                                               slot=i, n_slots=depth // 2, tq=512, precise=pm)
            kv_stacks = (knt_all, vt_all)
            xp = _outproj([o_p.reshape(Mp, c_width)], w_out_odd[i], xp, mp[2], tm=tm, rows_per_mod=Sp, precise=pm)
            q_s, k_s, lf_s = _qknorm_sample(zs, zts, q_norm_g[i], k_norm_g[i], bf_pad, G, E, n_heads=c_heads)
            v_s = zs[:, 2 * c_width:]
            o_s = _decode_attention(page_table, q_s, k_s, v_s, lf_s, ckt, cvt, clft, G,
                                    layer=i, n_heads=c_heads)
            xs = _outproj([o_s], w_out_odd[i], xs, ms[2], tm=Bs, rows_per_mod=Bs, precise=True)
            fp_l.append(lf_p)
            ks_l.append(k_s.reshape(Bs, 1, c_heads, c_dh))
            vs_l.append(v_s.reshape(Bs, 1, c_heads, c_dh))
            fs_l.append(lf_s[:, :c_heads].reshape(Bs, 1, c_heads))
        wr_pad = pad_cols(w_router[l])
        br_pad = jnp.zeros((1, LANE), F32).at[0, :N_EXPERTS].set(b_router[l])
        hp, lgp = _router(xp, g2, mp[4], mp[3], wr_pad, br_pad, tm=tm, rows_per_mod=Sp)
        hs, lgs = _router(xs, g2, ms[4], ms[3], wr_pad, br_pad, tm=Bs, rows_per_mod=Bs)
        h_all = jnp.concatenate([hp, hs], axis=0)
        lg_all = jnp.concatenate([lgp, lgs], axis=0)[:, :N_EXPERTS]
        ya, gt = _moe_ffn(h_all, lg_all, w_mlp1, b_mlp1, w_mlp2, b_mlp2, layer=l, precise=l < PRECISE_MOE_LAYERS)
        yp = jnp.sum(ya[:, :Mp] * gt[:, :Mp, None], axis=0)
        ys = jnp.sum(ya[:, Mp:] * gt[:, Mp:, None], axis=0)
        xp = xp + (mp[5] * yp.reshape(Bp, Sp, D)).reshape(Mp, D)
        xs = xs + ms[5][0] * ys

    def new_rows(t):
        return jnp.transpose(t.reshape(t.shape[0], Bp, c_heads, c_dh, Sp), (0, 1, 4, 2, 3))

    return (xp.reshape(Bp, Sp, D), xs.reshape(Bs, 1, D),
            new_rows(kv_stacks[0]), new_rows(kv_stacks[1]), jnp.stack(fp_l),
            jnp.stack(ks_l), jnp.stack(vs_l), jnp.stack(fs_l),
            jnp.stack(dp_l), jnp.stack(cp_l), jnp.stack(ds_l), jnp.stack(cs_l), jnp.stack(av_l))
```
